```python
import math
import jax
import jax.numpy as jnp
from jax import lax
import numpy as np

D_MODEL = 4096
BATCH = 1
SEQ = 16384
DEPTH = 4

CTX_LEN = 256
GRID_W = 64
N_MIXERS = 3
EPS = 1e-6
ADA_RANK = 512
D_FF = 10240
CONV_K = 3
S5_GROUP = 16
S5_GROUPS = D_MODEL // S5_GROUP
S5_STATE = 64
S5_CHUNK = 128
GLA_HEADS = 8
GLA_QK = D_MODEL // 2
GLA_V = D_MODEL
GLA_DK = GLA_QK // GLA_HEADS
GLA_DV = GLA_V // GLA_HEADS
GLA_GATE_RANK = 16
GLA_TAU = 16.0
GLA_CHUNK = 64
ML_HEADS = 8
ML_QK = D_MODEL // 2
ML_V = D_MODEL
ML_DK = ML_QK // ML_HEADS
ML_DV = ML_V // ML_HEADS
ML_CHUNK = 64
N_S5 = (DEPTH + 2) // 3
N_GLA = (DEPTH + 1) // 3
N_ML = DEPTH // 3

kernel_name = 'hybrid_s5_gla_mlstm_prefix_dit'


def _ident(t):
    return t


def _flip(t):
    return jnp.flip(t, axis=1)


def _split(z, sizes):
    return jnp.split(z, np.cumsum(sizes)[:-1].tolist(), axis=-1)


def _to_chunks(t, size):
    b, n = t.shape[0], t.shape[1] // size
    return jnp.moveaxis(t.reshape(b, n, size, *t.shape[2:]), 1, 0)


def _from_chunks(t):
    t = jnp.moveaxis(t, 0, 1)
    return t.reshape(t.shape[0], t.shape[1] * t.shape[2], *t.shape[3:])


def rms_norm(x, g):
    xf = x.astype(jnp.float32)
    y = xf * lax.rsqrt(jnp.mean(xf * xf, axis=-1, keepdims=True) + EPS)
    return (y * g.astype(jnp.float32)).astype(x.dtype)


def ada_mod(cond, w_down, w_up, b):
    m = (jax.nn.silu(cond) @ w_down) @ w_up + b
    return jnp.split(m, 6, axis=-1)


def _ssm_combine(e1, e2):
    a1, b1 = e1
    a2, b2 = e2
    return a1 * a2, a2 * b1 + b2


def s5_scan(u, h0, lam_bar, b_bar, c_mat, readout):
    def step(h, u_c):
        bu = jnp.einsum('btgi,gpi->btgp', u_c.astype(jnp.complex64), b_bar)
        a = jnp.broadcast_to(lam_bar, bu.shape)
        a_cum, h_loc = lax.associative_scan(_ssm_combine, (a, bu), axis=1)
        h_all = h_loc + a_cum * h[:, None]
        y = jnp.einsum('btgp,gip->btgi', h_all, c_mat).real if readout else None
        return h_all[:, -1], y
    h_last, y = lax.scan(step, h0, _to_chunks(u, S5_CHUNK))
    return (_from_chunks(y) if readout else None), h_last


def s5_mixer(u, uc, lam_re, lam_im, log_step, b_re, b_im, c_re, c_im, d_skip, w_glu, b_glu, ctx_out):
    f32 = jnp.float32

    def groups(t):
        return t.astype(f32).reshape(t.shape[0], t.shape[1], S5_GROUPS, S5_GROUP)

    y = d_skip * u
    yc = d_skip * uc if ctx_out else None
    for dr in range(2):
        flip = _flip if dr else _ident
        lam = lax.complex(lam_re[dr].astype(f32), lam_im[dr].astype(f32))
        lam_bar = jnp.exp(lam * jnp.exp(log_step[dr].astype(f32))[:, None])
        b_bar = ((lam_bar - 1.0) / lam)[..., None] * lax.complex(b_re[dr].astype(f32), b_im[dr].astype(f32))
        c_mat = lax.complex(c_re[dr].astype(f32), c_im[dr].astype(f32))
        h0 = jnp.zeros((uc.shape[0], S5_GROUPS, S5_STATE), jnp.complex64)
        y_c, h_c = s5_scan(groups(flip(uc)), h0, lam_bar, b_bar, c_mat, ctx_out)
        y_l, _ = s5_scan(groups(flip(u)), h_c, lam_bar, b_bar, c_mat, True)
        y = y + flip(y_l.reshape(u.shape))
        if ctx_out:
            yc = yc + flip(y_c.reshape(uc.shape))

    def glu(t):
        a, g = jnp.split(jax.nn.gelu(t) @ w_glu + b_glu, 2, axis=-1)
        return a * jax.nn.sigmoid(g)
    return glu(y), (glu(yc) if ctx_out else None)


def gla_scan(q, k, v, log_a, s0, readout):
    causal = jnp.tril(jnp.ones((GLA_CHUNK, GLA_CHUNK), dtype=bool))[None, :, :, None, None]

    def step(s, xs):
        qc, kc, vc, lac = xs
        b = jnp.cumsum(lac, axis=1)
        b_last = b[:, -1]
        s_new = jnp.exp(b_last)[..., None] * s + jnp.einsum('bshk,bshv->bhkv', kc * jnp.exp(b_last[:, None] - b), vc)
        if not readout:
            return s_new, None
        dec = jnp.exp(jnp.where(causal, b[:, :, None] - b[:, None, :], -jnp.inf))
        att = jnp.einsum('bthk,btshk,bshk->bhts', qc, dec, kc)
        o = jnp.einsum('bhts,bshv->bthv', att, vc) + jnp.einsum('bthk,bhkv->bthv', qc * jnp.exp(b), s)
        return s_new, o
    xs = tuple(_to_chunks(t, GLA_CHUNK) for t in (q, k, v, log_a))
    s_last, o = lax.scan(step, s0, xs)
    return (_from_chunks(o) if readout else None), s_last


def gla_mixer(u, uc, w_in, w_gate_up, b_gate, norm_g, w_out, ctx_out):
    f32 = jnp.float32

    def project(t):
        bsz, n = t.shape[0], t.shape[1]
        q, k, v, g, r_f, r_b = _split(t @ w_in, [GLA_QK, GLA_QK, GLA_V, GLA_V, GLA_GATE_RANK, GLA_GATE_RANK])

        def heads(a):
            return a.astype(f32).reshape(bsz, n, GLA_HEADS, -1)
        log_a = [heads(jax.nn.log_sigmoid(r @ w_gate_up[dr] + b_gate[dr]) / GLA_TAU)
                 for dr, r in enumerate((r_f, r_b))]
        return heads(q) * GLA_DK ** -0.5, heads(k), heads(v), g, log_a

    q, k, v, g, log_a = project(u)
    qc, kc, vc, gc, log_ac = project(uc)
    o = 0.0
    oc = 0.0
    for dr in range(2):
        flip = _flip if dr else _ident
        s0 = jnp.zeros((uc.shape[0], GLA_HEADS, GLA_DK, GLA_DV), f32)
        o_c, s_c = gla_scan(flip(qc), flip(kc), flip(vc), flip(log_ac[dr]), s0, ctx_out)
        o_l, _ = gla_scan(flip(q), flip(k), flip(v), flip(log_a[dr]), s_c, True)
        o = o + flip(o_l)
        if ctx_out:
            oc = oc + flip(o_c)

    def out(o_sum, gate):
        o_n = rms_norm(o_sum, norm_g).reshape(gate.shape[0], gate.shape[1], GLA_V)
        return (o_n * jax.nn.silu(gate)) @ w_out
    return out(o, g), (out(oc, gc) if ctx_out else None)


def mlstm_scan(q, k, v, log_i, log_f, state0, readout):
    causal = jnp.tril(jnp.ones((ML_CHUNK, ML_CHUNK), dtype=bool))[None, :, :, None]

    def step(state, xs):
        c0, n0, m0 = state
        qc, kc, vc, lic, lfc = xs
        b = jnp.cumsum(lfc, axis=1)
        logw = jnp.where(causal, b[:, :, None] - b[:, None] + lic[:, None], -jnp.inf)
        w_inter = b + m0[:, None]
        m_t = jnp.maximum(w_inter, jnp.max(logw, axis=2))
        m_new = m_t[:, -1]
        carry = jnp.exp(w_inter[:, -1] - m_new)
        kw = kc * jnp.exp(logw[:, -1] - m_new[:, None])[..., None]
        c_new = carry[..., None, None] * c0 + jnp.einsum('bshk,bshv->bhkv', kw, vc)
        n_new = carry[..., None] * n0 + jnp.sum(kw, axis=1)
        if not readout:
            return (c_new, n_new, m_new), None
        inter = jnp.exp(w_inter - m_t)
        qk = jnp.einsum('bthk,bshk->btsh', qc, kc) * jnp.exp(logw - m_t[:, :, None])
        num = jnp.einsum('btsh,bshv->bthv', qk, vc) + inter[..., None] * jnp.einsum('bthk,bhkv->bthv', qc, c0)
        den = jnp.sum(qk, axis=2) + inter * jnp.einsum('bthk,bhk->bth', qc, n0)
        h = num / jnp.maximum(jnp.abs(den), jnp.exp(-m_t))[..., None]
        return (c_new, n_new, m_new), h
    xs = tuple(_to_chunks(t, ML_CHUNK) for t in (q, k, v, log_i, log_f))
    state, h = lax.scan(step, state0, xs)
    return (_from_chunks(h) if readout else None), state


def mlstm_mixer(u, uc, w_in, b_gate, norm_g, w_out, ctx_out):
    f32 = jnp.float32

    def project(t):
        bsz, n = t.shape[0], t.shape[1]
        q, k, v, o, gates = _split(t @ w_in, [ML_QK, ML_QK, ML_V, ML_V, 4 * ML_HEADS])

        def heads(a):
            return a.astype(f32).reshape(bsz, n, ML_HEADS, -1)
        gates = gates.astype(f32).reshape(bsz, n, 4, ML_HEADS) + b_gate.astype(f32)
        dirs = [(gates[:, :, 2 * dr], jax.nn.log_sigmoid(gates[:, :, 2 * dr + 1])) for dr in range(2)]
        return heads(q), heads(k) * ML_DK ** -0.5, heads(v), o, dirs

    q, k, v, o, dirs = project(u)
    qc, kc, vc, oc_gate, dirs_c = project(uc)
    h = 0.0
    hc = 0.0
    for dr in range(2):
        flip = _flip if dr else _ident
        bsz = uc.shape[0]
        st0 = (jnp.zeros((bsz, ML_HEADS, ML_DK, ML_DV), f32), jnp.zeros((bsz, ML_HEADS, ML_DK), f32),
               jnp.zeros((bsz, ML_HEADS), f32))
        h_c, st_c = mlstm_scan(flip(qc), flip(kc), flip(vc), flip(dirs_c[dr][0]), flip(dirs_c[dr][1]), st0, ctx_out)
        h_l, _ = mlstm_scan(flip(q), flip(k), flip(v), flip(dirs[dr][0]), flip(dirs[dr][1]), st_c, True)
        h = h + flip(h_l)
        if ctx_out:
            hc = hc + flip(h_c)

    def out(h_sum, o_pre):
        h_n = rms_norm(h_sum, norm_g).reshape(o_pre.shape[0], o_pre.shape[1], ML_V)
        return (jax.nn.sigmoid(o_pre) * h_n) @ w_out
    return out(h, o), (out(hc, oc_gate) if ctx_out else None)


def conv_ffn(u, w_in, conv_w, conv_b, w_out, rows, cols):
    bsz, n = u.shape[0], u.shape[1]
    gate, up = jnp.split(u @ w_in, 2, axis=-1)
    g2 = gate.reshape(bsz, rows, cols, D_FF)
    g2 = lax.conv_general_dilated(g2, conv_w[:, :, None, :].astype(g2.dtype), window_strides=(1, 1),
                                  padding='SAME', dimension_numbers=('NHWC', 'HWIO', 'NHWC'),
                                  feature_group_count=D_FF)
    gate = g2.reshape(bsz, n, D_FF) + conv_b
    return (jax.nn.silu(gate) * up) @ w_out


def setup_inputs(seed: int = 0) -> dict:
    keys = iter(jax.random.split(jax.random.key(seed), 40))

    def nrm(shape, scale):
        return scale * jax.random.normal(next(keys), shape, jnp.float32)
    d = D_MODEL
    s5_shape = (N_S5, 2, S5_GROUPS, S5_STATE)
    f_bias = jnp.linspace(3.0, 6.0, ML_HEADS, dtype=jnp.float32)
    return {
        'x': nrm((BATCH, SEQ, d), 1.0),
        'c': nrm((BATCH, d), 1.0),
        'ctx': nrm((BATCH, CTX_LEN, d), 1.0),
        'c_ctx': nrm((d,), 1.0),
        'ada_w_down': nrm((DEPTH, d, ADA_RANK), d ** -0.5),
        'ada_w_up': nrm((DEPTH, ADA_RANK, 6 * d), 0.5 * ADA_RANK ** -0.5),
        'ada_b': nrm((DEPTH, 6 * d), 0.02),
        'norm_g': 1.0 + nrm((DEPTH, 4, d), 0.02),
        'ffn_w_in': nrm((DEPTH, d, 2 * D_FF), d ** -0.5),
        'ffn_conv_w': nrm((DEPTH, CONV_K, CONV_K, D_FF), 1.0 / CONV_K),
        'ffn_conv_b': nrm((DEPTH, D_FF), 0.02),
        'ffn_w_out': nrm((DEPTH, D_FF, d), D_FF ** -0.5),
        's5_lam_re': -0.5 + nrm(s5_shape, 0.02),
        's5_lam_im': math.pi * jnp.arange(S5_STATE, dtype=jnp.float32) + nrm(s5_shape, 0.02),
        's5_log_step': jax.random.uniform(next(keys), (N_S5, 2, S5_GROUPS), jnp.float32,
                                          math.log(1e-3), math.log(1e-1)),
        's5_b_re': nrm((N_S5, 2, S5_GROUPS, S5_STATE, S5_GROUP), (2 * S5_GROUP) ** -0.5),
        's5_b_im': nrm((N_S5, 2, S5_GROUPS, S5_STATE, S5_GROUP), (2 * S5_GROUP) ** -0.5),
        's5_c_re': nrm((N_S5, 2, S5_GROUPS, S5_GROUP, S5_STATE), S5_STATE ** -0.5),
        's5_c_im': nrm((N_S5, 2, S5_GROUPS, S5_GROUP, S5_STATE), S5_STATE ** -0.5),
        's5_d': nrm((N_S5, d), 0.5),
        's5_w_glu': nrm((N_S5, d, 2 * d), d ** -0.5),
        's5_b_glu': nrm((N_S5, 2 * d), 0.02),
        'gla_w_in': nrm((N_GLA, d, 2 * GLA_QK + 2 * GLA_V + 2 * GLA_GATE_RANK), d ** -0.5),
        'gla_w_gate_up': nrm((N_GLA, 2, GLA_GATE_RANK, GLA_QK), GLA_GATE_RANK ** -0.5),
        'gla_b_gate': nrm((N_GLA, 2, GLA_QK), 0.1),
        'gla_norm_g': 1.0 + nrm((N_GLA, GLA_DV), 0.02),
        'gla_w_out': nrm((N_GLA, GLA_V, d), GLA_V ** -0.5),
        'mlstm_w_in': nrm((N_ML, d, 2 * ML_QK + 2 * ML_V + 4 * ML_HEADS), d ** -0.5),
        'mlstm_b_gate': jnp.stack([nrm((N_ML, ML_HEADS), 0.1), f_bias + nrm((N_ML, ML_HEADS), 0.1),
                                   nrm((N_ML, ML_HEADS), 0.1), f_bias + nrm((N_ML, ML_HEADS), 0.1)], axis=1),
        'mlstm_norm_g': 1.0 + nrm((N_ML, ML_DV), 0.02),
        'mlstm_w_out': nrm((N_ML, ML_V, d), ML_V ** -0.5),
    }


def reference(x, c, ctx, c_ctx, ada_w_down, ada_w_up, ada_b, norm_g, ffn_w_in, ffn_conv_w, ffn_conv_b,
              ffn_w_out, s5_lam_re, s5_lam_im, s5_log_step, s5_b_re, s5_b_im, s5_c_re, s5_c_im, s5_d,
              s5_w_glu, s5_b_glu, gla_w_in, gla_w_gate_up, gla_b_gate, gla_norm_g, gla_w_out,
              mlstm_w_in, mlstm_b_gate, mlstm_norm_g, mlstm_w_out):
    rows = x.shape[1] // GRID_W
    ctx_len = ctx.shape[1]
    h, hc = x, ctx
    for i in range(DEPTH):
        kind, j = i % N_MIXERS, i // N_MIXERS
        ctx_out = i < DEPTH - 1
        sa, ca, ga, sf, cf, gf = [m[:, None, :] for m in ada_mod(c, ada_w_down[i], ada_w_up[i], ada_b[i])]
        sac, cac, gac, sfc, cfc, gfc = ada_mod(c_ctx, ada_w_down[i], ada_w_up[i], ada_b[i])
        u = rms_norm(h, norm_g[i, 0]) * (1.0 + ca) + sa
        uc = rms_norm(hc, norm_g[i, 0]) * (1.0 + cac) + sac
        if kind == 0:
            y, yc = s5_mixer(u, uc, s5_lam_re[j], s5_lam_im[j], s5_log_step[j], s5_b_re[j], s5_b_im[j],
                             s5_c_re[j], s5_c_im[j], s5_d[j], s5_w_glu[j], s5_b_glu[j], ctx_out)
        elif kind == 1:
            y, yc = gla_mixer(u, uc, gla_w_in[j], gla_w_gate_up[j], gla_b_gate[j], gla_norm_g[j],
                              gla_w_out[j], ctx_out)
        else:
            y, yc = mlstm_mixer(u, uc, mlstm_w_in[j], mlstm_b_gate[j], mlstm_norm_g[j], mlstm_w_out[j], ctx_out)
        h = h + ga * rms_norm(y, norm_g[i, 1])
        uf = rms_norm(h, norm_g[i, 2]) * (1.0 + cf) + sf
        h = h + gf * rms_norm(conv_ffn(uf, ffn_w_in[i], ffn_conv_w[i], ffn_conv_b[i], ffn_w_out[i], rows, GRID_W),
                              norm_g[i, 3])
        if ctx_out:
            hc = hc + gac * rms_norm(yc, norm_g[i, 1])
            ufc = rms_norm(hc, norm_g[i, 2]) * (1.0 + cfc) + sfc
            hc = hc + gfc * rms_norm(conv_ffn(ufc, ffn_w_in[i], ffn_conv_w[i], ffn_conv_b[i], ffn_w_out[i],
                                              1, ctx_len), norm_g[i, 3])
    return h
```

```python
import functools
import math

import numpy as np
import jax
import jax.numpy as jnp
from jax import lax
from jax.experimental import pallas as pl
from jax.experimental.pallas import tpu as pltpu

EPS = 1e-6
GRID_W = 64
S5_GROUP = 16
S5_T = 16
S5_GROUP_BLOCK = 4
HEADS = 8
CHUNK = 64
GLA_TAU = 16.0
GATE_PAD = 128
ROW_TILE = 256
MXU_DTYPE = jnp.bfloat16
V7X_VMEM_LIMIT_BYTES = 48 * 1024 * 1024

F32 = jnp.float32


def _cparams(*sem):
    return pltpu.CompilerParams(dimension_semantics=sem, vmem_limit_bytes=V7X_VMEM_LIMIT_BYTES)


def _pick(dim, candidates):
    for c in candidates:
        if dim % c == 0:
            return c
    raise ValueError(f"no tile for {dim} in {candidates}")


def _log_sigmoid(x):
    return jnp.minimum(x, 0.0) - jnp.log1p(jnp.exp(-jnp.abs(x)))


def _sigmoid(x):
    return 1.0 / (1.0 + jnp.exp(-x))


def _silu(x):
    return x * _sigmoid(x)


def _gelu_tanh(x):
    return 0.5 * x * (1.0 + jnp.tanh(math.sqrt(2.0 / math.pi) * (x + 0.044715 * (x * x * x))))


def _ada_kernel(cond_ref, wd_ref, wu_ref, b_ref, o_ref):
    s = _silu(cond_ref[...])
    t = jnp.dot(s, wd_ref[0], preferred_element_type=F32, precision=lax.Precision.HIGHEST)
    o_ref[0] = jnp.dot(t, wu_ref[0], preferred_element_type=F32, precision=lax.Precision.HIGHEST) + b_ref[0]


def ada_all(cond, w_down, w_up, b):
    depth, d, rank = w_down.shape
    n = w_up.shape[2]
    tn = _pick(n, (2048, 1024, 512, 256, 128))
    return pl.pallas_call(
        _ada_kernel,
        grid=(depth, n // tn),
        in_specs=[
            pl.BlockSpec((8, d), lambda l, j: (0, 0)),
            pl.BlockSpec((1, d, rank), lambda l, j: (l, 0, 0)),
            pl.BlockSpec((1, rank, tn), lambda l, j: (l, 0, j)),
            pl.BlockSpec((1, 1, tn), lambda l, j: (l, 0, j)),
        ],
        out_specs=pl.BlockSpec((1, 8, tn), lambda l, j: (l, 0, j)),
        out_shape=jax.ShapeDtypeStruct((depth, 8, n), F32),
        compiler_params=_cparams("arbitrary", "arbitrary"),
    )(cond, w_down, w_up, b.reshape(depth, 1, n))


def _rms(x, g):
    return x * lax.rsqrt(jnp.mean(x * x, axis=-1, keepdims=True) + EPS) * g


def _norm_mod_kernel(x_ref, g_ref, sc_ref, sh_ref, o_ref):
    y = _rms(x_ref[...], g_ref[...])
    o_ref[...] = (y * (1.0 + sc_ref[0]) + sh_ref[0]).astype(o_ref.dtype)


def _row_specs(d, ctx_tiles):
    row = pl.BlockSpec((ROW_TILE, d), lambda i: (i, 0))
    vec = pl.BlockSpec((1, d), lambda i: (0, 0))
    mod = pl.BlockSpec((1, 1, d), lambda i: (jnp.where(i < ctx_tiles, 0, 1), 0, 0))
    return row, vec, mod


def norm_mod(x, g, scale, shift, ctx_len, out_dtype):
    m, d = x.shape
    row, vec, mod = _row_specs(d, ctx_len // ROW_TILE)
    return pl.pallas_call(
        _norm_mod_kernel,
        grid=(m // ROW_TILE,),
        in_specs=[row, vec, mod, mod],
        out_specs=row,
        out_shape=jax.ShapeDtypeStruct((m, d), out_dtype),
        compiler_params=_cparams("parallel"),
    )(x, g.reshape(1, d), scale, shift)


def _resid_kernel(h_ref, y_ref, g_ref, gate_ref, o_ref):
    o_ref[...] = h_ref[...] + gate_ref[0] * _rms(y_ref[...].astype(F32), g_ref[...])


def resid(h, y, g, gate, ctx_len):
    m, d = h.shape
    row, vec, mod = _row_specs(d, ctx_len // ROW_TILE)
    return pl.pallas_call(
        _resid_kernel,
        grid=(m // ROW_TILE,),
        in_specs=[row, row, vec, mod],
        out_specs=row,
        out_shape=jax.ShapeDtypeStruct((m, d), F32),
        compiler_params=_cparams("parallel"),
    )(h, y, g.reshape(1, d), gate)


def _resid_glu_kernel(h_ref, a_ref, gl_ref, g_ref, gate_ref, o_ref):
    y = a_ref[...] * _sigmoid(gl_ref[...])
    o_ref[...] = h_ref[...] + gate_ref[0] * _rms(y, g_ref[...])


def resid_glu(h, ag, g, gate, ctx_len):
    m, d = h.shape
    row, vec, mod = _row_specs(d, ctx_len // ROW_TILE)
    return pl.pallas_call(
        _resid_glu_kernel,
        grid=(m // ROW_TILE,),
        in_specs=[row, row, pl.BlockSpec((ROW_TILE, d), lambda i: (i, 1)), vec, mod],
        out_specs=row,
        out_shape=jax.ShapeDtypeStruct((m, d), F32),
        compiler_params=_cparams("parallel"),
    )(h, ag, ag, g.reshape(1, d), gate)


def _s5_act_kernel(u_ref, y_ref, d_ref, o_ref):
    o_ref[...] = _gelu_tanh(d_ref[...] * u_ref[...] + y_ref[...]).astype(o_ref.dtype)


def s5_act(u, y, d_skip):
    m, d = u.shape
    row = pl.BlockSpec((ROW_TILE, d), lambda i: (i, 0))
    return pl.pallas_call(
        _s5_act_kernel,
        grid=(m // ROW_TILE,),
        in_specs=[row, row, pl.BlockSpec((1, d), lambda i: (0, 0))],
        out_specs=row,
        out_shape=jax.ShapeDtypeStruct((m, d), MXU_DTYPE),
        compiler_params=_cparams("parallel"),
    )(u, y, d_skip.reshape(1, d))


def _head_gate_kernel(o_ref, gate_ref, g_ref, out_ref, *, heads, sigmoid_gate):
    o = o_ref[0] + o_ref[1]
    dv = o.shape[1] // heads
    for h in range(heads):
        sl = slice(h * dv, (h + 1) * dv)
        gt = gate_ref[:, sl].astype(F32)
        act = _sigmoid(gt) if sigmoid_gate else _silu(gt)
        out_ref[:, sl] = (_rms(o[:, sl], g_ref[...]) * act).astype(out_ref.dtype)


def head_gate(o2, proj, gate_col_block, norm_g, sigmoid_gate):
    _, m, d = o2.shape
    dv = d // HEADS
    return pl.pallas_call(
        functools.partial(_head_gate_kernel, heads=HEADS, sigmoid_gate=sigmoid_gate),
        grid=(m // ROW_TILE,),
        in_specs=[
            pl.BlockSpec((2, ROW_TILE, d), lambda i: (0, i, 0)),
            pl.BlockSpec((ROW_TILE, d), lambda i: (i, gate_col_block)),
            pl.BlockSpec((1, dv), lambda i: (0, 0)),
        ],
        out_specs=pl.BlockSpec((ROW_TILE, d), lambda i: (i, 0)),
        out_shape=jax.ShapeDtypeStruct((m, d), MXU_DTYPE),
        compiler_params=_cparams("parallel"),
    )(o2, proj, norm_g.reshape(1, dv))


def _mm_kernel(a_ref, w_ref, b_ref, o_ref, acc_ref, *, nk):
    k = pl.program_id(2)

    @pl.when(k == 0)
    def _():
        acc_ref[...] = jnp.zeros_like(acc_ref)

    acc_ref[...] += jnp.dot(a_ref[...].astype(MXU_DTYPE), w_ref[...], preferred_element_type=F32)

    @pl.when(k == nk - 1)
    def _():
        o_ref[...] = (acc_ref[...] + b_ref[...]).astype(o_ref.dtype)


def matmul(a, w, bias, out_dtype):
    m, k = a.shape
    n = w.shape[1]
    tm = _pick(m, (1280, 1024, 640, 512, 256))
    tn = _pick(n, (1024, 512, 256, 128))
    tk = _pick(k, (1024, 512, 256, 128))
    nk = k // tk
    return pl.pallas_call(
        functools.partial(_mm_kernel, nk=nk),
        grid=(m // tm, n // tn, nk),
        in_specs=[
            pl.BlockSpec((tm, tk), lambda i, j, kk: (i, kk)),
            pl.BlockSpec((tk, tn), lambda i, j, kk: (kk, j)),
            pl.BlockSpec((1, tn), lambda i, j, kk: (0, j)),
        ],
        out_specs=pl.BlockSpec((tm, tn), lambda i, j, kk: (i, j)),
        out_shape=jax.ShapeDtypeStruct((m, n), out_dtype),
        scratch_shapes=[pltpu.VMEM((tm, tn), F32)],
        compiler_params=_cparams("parallel", "parallel", "arbitrary"),
    )(a, w, bias.reshape(1, n).astype(F32))


def _conv_act_kernel(gp_ref, gc_ref, gn_ref, up_ref, w_ref, b_ref, o_ref, *, tb, ctx_len, n_lat, grid_w):
    i = pl.program_id(0)
    g = jnp.concatenate([gp_ref[...], gc_ref[...], gn_ref[...]], axis=0).astype(F32)
    n_ext = tb + 2 * grid_w
    tok = i * tb + lax.broadcasted_iota(jnp.int32, (tb, 1), 0)
    is_ctx = tok < ctx_len
    lt = tok - ctx_len
    col = jnp.where(is_ctx, tok, jnp.bitwise_and(lt, grid_w - 1))
    ncols = jnp.where(is_ctx, ctx_len, grid_w)
    is_lat = jnp.logical_not(is_ctx)
    row_ok = {-1: jnp.logical_and(is_lat, lt >= grid_w), 0: None,
              1: jnp.logical_and(is_lat, lt < n_lat - grid_w)}
    col_ok = {-1: col > 0, 0: None, 1: col < ncols - 1}
    acc = jnp.zeros((tb, g.shape[1]), F32)
    for dc in (-1, 0, 1):
        src = g if dc == 0 else pltpu.roll(g, (-dc) % n_ext, axis=0)
        for dr in (-1, 0, 1):
            base = src[grid_w + dr * grid_w: grid_w + dr * grid_w + tb]
            ok = row_ok[dr]
            if col_ok[dc] is not None:
                ok = col_ok[dc] if ok is None else jnp.logical_and(ok, col_ok[dc])
            if ok is not None:
                base = jnp.where(ok, base, 0.0)
            acc = acc + w_ref[pl.ds(3 * (dr + 1) + (dc + 1), 1), :] * base
    gate = acc + b_ref[...]
    o_ref[...] = (_silu(gate) * up_ref[...].astype(F32)).astype(o_ref.dtype)


def conv_act(gu, conv_w, conv_b, ctx_len):
    m, f2 = gu.shape
    f = f2 // 2
    tb = ROW_TILE
    assert GRID_W & (GRID_W - 1) == 0 and tb % GRID_W == 0 and ctx_len == tb
    tc = _pick(f, (1024, 512, 256, 128))
    nblk = m // GRID_W
    r = tb // GRID_W
    kern = functools.partial(_conv_act_kernel, tb=tb, ctx_len=ctx_len, n_lat=m - ctx_len, grid_w=GRID_W)
    return pl.pallas_call(
        kern,
        grid=(m // tb, f // tc),
        in_specs=[
            pl.BlockSpec((GRID_W, tc), lambda i, j: (jnp.maximum(i * r - 1, 0), j)),
            pl.BlockSpec((tb, tc), lambda i, j: (i, j)),
            pl.BlockSpec((GRID_W, tc), lambda i, j: (jnp.minimum((i + 1) * r, nblk - 1), j)),
            pl.BlockSpec((tb, tc), lambda i, j: (i, j + f // tc)),
            pl.BlockSpec((9, tc), lambda i, j: (0, j)),
            pl.BlockSpec((1, tc), lambda i, j: (0, j)),
        ],
        out_specs=pl.BlockSpec((tb, tc), lambda i, j: (i, j)),
        out_shape=jax.ShapeDtypeStruct((m, f), MXU_DTYPE),
        compiler_params=_cparams("parallel", "parallel"),
    )(gu, gu, gu, gu, conv_w.reshape(9, f), conv_b.reshape(1, f))


def _s5_weights(lam_re, lam_im, log_step, b_re, b_im, c_re, c_im):
    t = S5_T
    lam = lax.complex(lam_re.astype(F32), lam_im.astype(F32))
    step = jnp.exp(log_step.astype(F32))[:, :, None]
    pw = jnp.exp(jnp.arange(t + 1, dtype=F32)[:, None, None, None] * (lam * step)[None])
    lam_bar = pw[1]
    b_bar = ((lam_bar - 1.0) / lam)[..., None] * lax.complex(b_re.astype(F32), b_im.astype(F32))
    c_mat = lax.complex(c_re.astype(F32), c_im.astype(F32))
    g_n, p_n, i_n = b_bar.shape[1], b_bar.shape[2], b_bar.shape[3]
    kern = jnp.real(jnp.einsum('dgip,tdgp,dgpj->tdgij', c_mat, pw[:t], b_bar,
                               precision=lax.Precision.HIGHEST))
    s_in = np.arange(t)[:, None]
    s_out = np.arange(t)[None, :]
    lag_f = s_out - s_in
    lag_b = s_in - s_out
    wf = kern[np.clip(lag_f, 0, t - 1), 0] * jnp.asarray(lag_f >= 0, F32)[:, :, None, None, None]
    wb = kern[np.clip(lag_b, 0, t - 1), 1] * jnp.asarray(lag_b >= 0, F32)[:, :, None, None, None]
    w_intra = jnp.transpose(wf + wb, (2, 0, 4, 1, 3)).reshape(g_n, t * i_n, t * i_n)
    vf = pw[t - 1 - np.arange(t), 0][..., None] * b_bar[0][None]
    vb = pw[np.arange(t), 1][..., None] * b_bar[1][None]
    w_in = jnp.concatenate([jnp.real(vf), jnp.real(vb), jnp.imag(vf), jnp.imag(vb)], axis=2)
    w_in = jnp.transpose(w_in, (1, 0, 3, 2)).reshape(g_n, t * i_n, 4 * p_n)
    qf = c_mat[0][None] * pw[1 + np.arange(t), 0][:, :, None, :]
    qb = c_mat[1][None] * pw[t - np.arange(t), 1][:, :, None, :]
    w_out = jnp.concatenate([jnp.real(qf), jnp.real(qb), -jnp.imag(qf), -jnp.imag(qb)], axis=3)
    w_out = jnp.transpose(w_out, (1, 3, 0, 2)).reshape(g_n, 4 * p_n, t * i_n)
    a16 = pw[t]
    a_re = jnp.concatenate([jnp.real(a16[0]), jnp.real(a16[1])], axis=-1)[:, None, :]
    a_im = jnp.concatenate([jnp.imag(a16[0]), jnp.imag(a16[1])], axis=-1)[:, None, :]
    return (w_in.astype(MXU_DTYPE), w_intra.astype(MXU_DTYPE), w_out.astype(MXU_DTYPE), a_re, a_im)


def _s5_kernel(u_ref, win_ref, wintra_ref, wout_ref, ar_ref, ai_ref, y_ref,
               z_ref, hrf_ref, hrb_ref, hif_ref, hib_ref, *, gb, n_rows, n_ctx_rows, p2):
    half = p2 // 2
    for g in range(gb):
        z_ref[g] = jnp.dot(u_ref[g], win_ref[g], preferred_element_type=F32)
    is_f = lax.broadcasted_iota(jnp.int32, (1, p2), 1) < half
    a_r = [ar_ref[g] for g in range(gb)]
    a_i = [ai_ref[g] for g in range(gb)]

    sub = 8
    n_tiles = n_rows // sub
    n_ctx_tiles = n_ctx_rows // sub

    def step(i, carry):
        tf = pl.multiple_of(i * sub, sub)
        tb = pl.multiple_of(
            jnp.where(i < n_ctx_tiles, n_ctx_tiles - 1 - i, n_tiles - 1 - (i - n_ctx_tiles)) * sub, sub)
        out = []
        for g in range(gb):
            h_r, h_i = carry[2 * g], carry[2 * g + 1]
            zrf, zrb = z_ref[g, pl.ds(tf, sub), 0:p2], z_ref[g, pl.ds(tb, sub), 0:p2]
            zif, zib = z_ref[g, pl.ds(tf, sub), p2:2 * p2], z_ref[g, pl.ds(tb, sub), p2:2 * p2]
            rows_r, rows_i = [], []
            for j in range(sub):
                rows_r.append(h_r)
                rows_i.append(h_i)
                z_r = jnp.where(is_f, zrf[j:j + 1], zrb[sub - 1 - j:sub - j])
                z_i = jnp.where(is_f, zif[j:j + 1], zib[sub - 1 - j:sub - j])
                h_r, h_i = a_r[g] * h_r - a_i[g] * h_i + z_r, a_r[g] * h_i + a_i[g] * h_r + z_i
            hrf_ref[g, pl.ds(tf, sub), :] = jnp.concatenate(rows_r, axis=0)
            hrb_ref[g, pl.ds(tb, sub), :] = jnp.concatenate(rows_r[::-1], axis=0)
            hif_ref[g, pl.ds(tf, sub), :] = jnp.concatenate(rows_i, axis=0)
            hib_ref[g, pl.ds(tb, sub), :] = jnp.concatenate(rows_i[::-1], axis=0)
            out += [h_r, h_i]
        return tuple(out)

    zero = jnp.zeros((1, p2), F32)
    lax.fori_loop(0, n_tiles, step, tuple(zero for _ in range(2 * gb)))
    for g in range(gb):
        h_r = jnp.where(is_f, hrf_ref[g], hrb_ref[g])
        h_i = jnp.where(is_f, hif_ref[g], hib_ref[g])
        hp = jnp.concatenate([h_r, h_i], axis=1).astype(MXU_DTYPE)
        y_ref[g] = (jnp.dot(u_ref[g], wintra_ref[g], preferred_element_type=F32)
                    + jnp.dot(hp, wout_ref[g], preferred_element_type=F32))


def s5_scan(u, weights, ctx_len):
    m, d = u.shape
    w_in, w_intra, w_out, a_re, a_im = weights
    g_n = d // S5_GROUP
    n_rows = m // S5_T
    width = S5_T * S5_GROUP
    p2 = a_re.shape[-1]
    gb = S5_GROUP_BLOCK
    assert n_rows % 8 == 0 and (ctx_len // S5_T) % 8 == 0 and g_n % gb == 0
    ug = u.astype(MXU_DTYPE).reshape(n_rows, S5_T, g_n, S5_GROUP)
    ug = jnp.transpose(ug, (2, 0, 1, 3)).reshape(g_n, n_rows, width)
    kern = functools.partial(_s5_kernel, gb=gb, n_rows=n_rows, n_ctx_rows=ctx_len // S5_T, p2=p2)
    blk = lambda a, b: pl.BlockSpec((gb, a, b), lambda i: (i, 0, 0))
    yg = pl.pallas_call(
        kern,
        grid=(g_n // gb,),
        in_specs=[blk(n_rows, width), blk(width, 2 * p2), blk(width, width), blk(2 * p2, width),
                  blk(1, p2), blk(1, p2)],
        out_specs=blk(n_rows, width),
        out_shape=jax.ShapeDtypeStruct((g_n, n_rows, width), F32),
        scratch_shapes=[pltpu.VMEM((gb, n_rows, 2 * p2), F32)] + [pltpu.VMEM((gb, n_rows, p2), F32)] * 4,
        compiler_params=_cparams("parallel"),
    )(ug, w_in, w_intra, w_out, a_re, a_im)
    yg = yg.reshape(g_n, n_rows, S5_T, S5_GROUP)
    return jnp.transpose(yg, (1, 2, 0, 3)).reshape(m, d)


def _chunk_index(d, c, n_ctx_chunks, n_chunks):
    back = jnp.where(c < n_ctx_chunks, n_ctx_chunks - 1 - c, n_chunks - 1 - (c - n_ctx_chunks))
    return jnp.where(d == 0, c, back)


def _tri_masks():
    lower = np.tril(np.ones((CHUNK, CHUNK), np.float32))
    return jnp.asarray(np.stack([np.stack([lower, lower.T]), np.stack([lower.T, lower])]))


def _split_hi_lo(x):
    hi = x.astype(jnp.bfloat16)
    lo = (x - hi.astype(F32)).astype(jnp.bfloat16)
    return hi, lo


def _gla_kernel(q_ref, k_ref, v_ref, r_ref, wg_ref, bg_ref, tri_ref, o_ref, s_ref, *, dk):
    @pl.when(pl.program_id(2) == 0)
    def _():
        s_ref[...] = jnp.zeros_like(s_ref)

    x = jnp.dot(r_ref[...].astype(MXU_DTYPE), wg_ref[0].astype(MXU_DTYPE), preferred_element_type=F32) + bg_ref[0]
    la = _log_sigmoid(x) * (1.0 / GLA_TAU)
    tri = tri_ref[0, 0]
    hi, lo = _split_hi_lo(la)
    hl = jnp.concatenate([hi, lo], axis=1)
    bb = jnp.dot(tri.astype(jnp.bfloat16), hl, preferred_element_type=F32)
    b = bb[:, :dk] + bb[:, dk:]
    tot = jnp.sum(la, axis=0, keepdims=True)
    ones = jnp.ones((la.shape[0], 128), jnp.bfloat16)
    tc = lax.dot_general(hl, ones, (((0,), (0,)), ((), ())), preferred_element_type=F32)
    dec_col = jnp.exp(tc[:dk] + tc[dk:])
    q = q_ref[...].astype(F32) * dk ** -0.5
    k = k_ref[...].astype(F32)
    v = v_ref[...].astype(MXU_DTYPE)
    mid = 0.5 * tot
    qe = (q * jnp.exp(b - mid)).astype(MXU_DTYPE)
    ke = (k * jnp.exp(mid - b)).astype(MXU_DTYPE)
    att = lax.dot_general(qe, ke, (((1,), (1,)), ((), ())), preferred_element_type=F32)
    att = jnp.where(tri > 0, att, 0.0)
    s = s_ref[...]
    qb = (q * jnp.exp(b)).astype(MXU_DTYPE)
    o_ref[0] = (jnp.dot(att.astype(MXU_DTYPE), v, preferred_element_type=F32)
                + jnp.dot(qb, s.astype(MXU_DTYPE), preferred_element_type=F32))
    kd = (k * jnp.exp(tot - b)).astype(MXU_DTYPE)
    upd = lax.dot_general(kd, v, (((0,), (0,)), ((), ())), preferred_element_type=F32)
    dv = s.shape[1]
    s_ref[...] = s * jnp.concatenate([dec_col] * (dv // 128), axis=1) + upd


def gla_scan(proj, r, w_gate_up, b_gate, ctx_len):
    m = proj.shape[0]
    n_dir, rank, qk = w_gate_up.shape
    dk = qk // HEADS
    dv = 2 * dk
    n_chunks = m // CHUNK
    n_ctx = ctx_len // CHUNK
    wg = jnp.zeros((n_dir, GATE_PAD, qk), F32)
    for dr in range(n_dir):
        wg = wg.at[dr, dr * rank:(dr + 1) * rank].set(w_gate_up[dr])
    cidx = functools.partial(_chunk_index, n_ctx_chunks=n_ctx, n_chunks=n_chunks)
    return pl.pallas_call(
        functools.partial(_gla_kernel, dk=dk),
        grid=(n_dir, HEADS, n_chunks),
        in_specs=[
            pl.BlockSpec((CHUNK, dk), lambda d, h, c: (cidx(d, c), h)),
            pl.BlockSpec((CHUNK, dk), lambda d, h, c: (cidx(d, c), HEADS + h)),
            pl.BlockSpec((CHUNK, dv), lambda d, h, c: (cidx(d, c), HEADS + h)),
            pl.BlockSpec((CHUNK, GATE_PAD), lambda d, h, c: (cidx(d, c), 0)),
            pl.BlockSpec((1, GATE_PAD, dk), lambda d, h, c: (d, 0, h)),
            pl.BlockSpec((1, 1, dk), lambda d, h, c: (d, 0, h)),
            pl.BlockSpec((1, 1, CHUNK, CHUNK), lambda d, h, c: (d, 0, 0, 0)),
        ],
        out_specs=pl.BlockSpec((1, CHUNK, dv), lambda d, h, c: (d, cidx(d, c), h)),
        out_shape=jax.ShapeDtypeStruct((n_dir, m, dv * HEADS), F32),
        scratch_shapes=[pltpu.VMEM((dk, dv), F32)],
        compiler_params=_cparams("arbitrary", "arbitrary", "arbitrary"),
    )(proj, proj, proj, r, wg, b_gate.reshape(n_dir, 1, qk), _tri_masks())


def _mlstm_kernel(q_ref, k_ref, v_ref, gcol_ref, grow_ref, tri_ref, o_ref, c_ref, m_ref, *, dk, dv):
    d = pl.program_id(0)
    h = pl.program_id(2)

    @pl.when(pl.program_id(1) == 0)
    def _():
        c_ref[h] = jnp.zeros(c_ref.shape[1:], F32)
        m_ref[h] = jnp.zeros(m_ref.shape[1:], F32)

    t = q_ref.shape[0]
    tri = tri_ref[0, 0]
    tri_t = tri_ref[0, 1]
    vis = tri > 0
    col_i = 2 * d * HEADS + h
    lane = lax.broadcasted_iota(jnp.int32, (1, gcol_ref.shape[1]), 1)
    gcol = gcol_ref[...]
    li_c = jnp.sum(jnp.where(lane == col_i, gcol, 0.0), axis=1, keepdims=True)
    lf_c = _log_sigmoid(jnp.sum(jnp.where(lane == col_i + HEADS, gcol, 0.0), axis=1, keepdims=True))
    li_r = grow_ref[0, pl.ds(col_i, 1), :]
    lf_r = _log_sigmoid(grow_ref[0, pl.ds(col_i + HEADS, 1), :])
    b_c = jnp.sum(tri * lf_r, axis=1, keepdims=True)
    b_r = jnp.sum(tri_t * lf_c, axis=0, keepdims=True)
    b_last = jnp.sum(lf_r, axis=1, keepdims=True)
    m0 = m_ref[h][:, 0:1]
    logw = jnp.where(vis, b_c - b_r + li_r, -jnp.inf)
    w_inter = b_c + m0
    m_t = jnp.maximum(w_inter, jnp.max(logw, axis=1, keepdims=True))
    row = lax.broadcasted_iota(jnp.int32, (t, 1), 0)
    last = jnp.where(d == 0, t - 1, 0)
    m_new = jnp.max(jnp.where(row == last, m_t, -jnp.inf), axis=0, keepdims=True)
    carry = jnp.exp(b_last + m0 - m_new)
    w_last_c = jnp.exp(b_last - b_c + li_c - m_new)
    q = q_ref[...].astype(MXU_DTYPE)
    kf = k_ref[...].astype(F32) * dk ** -0.5
    kb = kf.astype(MXU_DTYPE)
    e0 = (lax.broadcasted_iota(jnp.int32, (t, 128), 1) == 0).astype(MXU_DTYPE)
    v_ext = jnp.concatenate([v_ref[...].astype(MXU_DTYPE), e0], axis=1)
    c0 = c_ref[h]
    qk = lax.dot_general(q, kb, (((1,), (1,)), ((), ())), preferred_element_type=F32)
    qk = qk * jnp.exp(logw - m_t)
    inter = jnp.exp(w_inter - m_t)
    num = (jnp.dot(qk.astype(MXU_DTYPE), v_ext, preferred_element_type=F32)
           + inter * jnp.dot(q, c0.astype(MXU_DTYPE), preferred_element_type=F32))
    den = num[:, dv:dv + 1]
    o_ref[0] = num[:, :dv] / jnp.maximum(jnp.abs(den), jnp.exp(-m_t))
    kw = (kf * w_last_c).astype(MXU_DTYPE)
    c_ref[h] = carry * c0 + lax.dot_general(kw, v_ext, (((0,), (0,)), ((), ())), preferred_element_type=F32)
    m_ref[h] = jnp.broadcast_to(m_new, m_ref.shape[1:])


def mlstm_scan(proj, gates, ctx_len):
    m = proj.shape[0]
    dk = proj.shape[1] // (6 * HEADS)
    dv = 2 * dk
    n_chunks = m // CHUNK
    n_ctx = ctx_len // CHUNK
    n_gate = 4 * HEADS
    grow = jnp.transpose(gates[:, :n_gate].reshape(n_chunks, CHUNK, n_gate), (0, 2, 1))
    cidx = functools.partial(_chunk_index, n_ctx_chunks=n_ctx, n_chunks=n_chunks)
    return pl.pallas_call(
        functools.partial(_mlstm_kernel, dk=dk, dv=dv),
        grid=(2, n_chunks, HEADS),
        in_specs=[
            pl.BlockSpec((CHUNK, dk), lambda d, c, h: (cidx(d, c), h)),
            pl.BlockSpec((CHUNK, dk), lambda d, c, h: (cidx(d, c), HEADS + h)),
            pl.BlockSpec((CHUNK, dv), lambda d, c, h: (cidx(d, c), HEADS + h)),
            pl.BlockSpec((CHUNK, GATE_PAD), lambda d, c, h: (cidx(d, c), 0)),
            pl.BlockSpec((1, n_gate, CHUNK), lambda d, c, h: (cidx(d, c), 0, 0)),
            pl.BlockSpec((1, 2, CHUNK, CHUNK), lambda d, c, h: (d, 0, 0, 0)),
        ],
        out_specs=pl.BlockSpec((1, CHUNK, dv), lambda d, c, h: (d, cidx(d, c), h)),
        out_shape=jax.ShapeDtypeStruct((2, m, dv * HEADS), F32),
        scratch_shapes=[pltpu.VMEM((HEADS, dk, dv + 128), F32), pltpu.VMEM((HEADS, 1, 128), F32)],
        compiler_params=_cparams("arbitrary", "arbitrary", "arbitrary"),
    )(proj, proj, proj, gates, grow, _tri_masks())


def _pad_cols(w, width):
    return jnp.pad(w, ((0, 0), (0, width - w.shape[1])))


def _s5_layer(u, p, j, ctx_len):
    weights = _s5_weights(p['s5_lam_re'][j], p['s5_lam_im'][j], p['s5_log_step'][j], p['s5_b_re'][j],
                          p['s5_b_im'][j], p['s5_c_re'][j], p['s5_c_im'][j])
    y = s5_scan(u, weights, ctx_len)
    act = s5_act(u, y, p['s5_d'][j])
    return matmul(act, p['s5_w_glu'][j].astype(MXU_DTYPE), p['s5_b_glu'][j], F32)


def _gla_layer(u, p, j, ctx_len):
    w_in = p['gla_w_in'][j]
    n_main = w_in.shape[1] - 2 * p['gla_w_gate_up'].shape[2]
    proj = matmul(u, w_in[:, :n_main].astype(MXU_DTYPE), jnp.zeros((n_main,), F32), MXU_DTYPE)
    r = matmul(u, _pad_cols(w_in[:, n_main:], GATE_PAD).astype(MXU_DTYPE), jnp.zeros((GATE_PAD,), F32), F32)
    o2 = gla_scan(proj, r, p['gla_w_gate_up'][j], p['gla_b_gate'][j], ctx_len)
    on = head_gate(o2, proj, 2, p['gla_norm_g'][j], sigmoid_gate=False)
    d = u.shape[1]
    return matmul(on, p['gla_w_out'][j].astype(MXU_DTYPE), jnp.zeros((d,), F32), F32)


def _mlstm_layer(u, p, j, ctx_len):
    w_in = p['mlstm_w_in'][j]
    n_gate = 4 * HEADS
    n_main = w_in.shape[1] - n_gate
    proj = matmul(u, w_in[:, :n_main].astype(MXU_DTYPE), jnp.zeros((n_main,), F32), MXU_DTYPE)
    b_gate = jnp.pad(p['mlstm_b_gate'][j].reshape(n_gate), (0, GATE_PAD - n_gate))
    gates = matmul(u, _pad_cols(w_in[:, n_main:], GATE_PAD).astype(MXU_DTYPE), b_gate, F32)
    h2 = mlstm_scan(proj, gates, ctx_len)
    hn = head_gate(h2, proj, 2, p['mlstm_norm_g'][j], sigmoid_gate=True)
    d = u.shape[1]
    return matmul(hn, p['mlstm_w_out'][j].astype(MXU_DTYPE), jnp.zeros((d,), F32), F32)


def kernel(x, c, ctx, c_ctx, ada_w_down, ada_w_up, ada_b, norm_g, ffn_w_in, ffn_conv_w, ffn_conv_b, ffn_w_out, s5_lam_re, s5_lam_im, s5_log_step, s5_b_re, s5_b_im, s5_c_re, s5_c_im, s5_d, s5_w_glu, s5_b_glu, gla_w_in, gla_w_gate_up, gla_b_gate, gla_norm_g, gla_w_out, mlstm_w_in, mlstm_b_gate, mlstm_norm_g, mlstm_w_out):
    p = dict(s5_lam_re=s5_lam_re, s5_lam_im=s5_lam_im, s5_log_step=s5_log_step, s5_b_re=s5_b_re, s5_b_im=s5_b_im,
             s5_c_re=s5_c_re, s5_c_im=s5_c_im, s5_d=s5_d, s5_w_glu=s5_w_glu, s5_b_glu=s5_b_glu,
             gla_w_in=gla_w_in, gla_w_gate_up=gla_w_gate_up, gla_b_gate=gla_b_gate, gla_norm_g=gla_norm_g,
             gla_w_out=gla_w_out, mlstm_w_in=mlstm_w_in, mlstm_b_gate=mlstm_b_gate, mlstm_norm_g=mlstm_norm_g,
             mlstm_w_out=mlstm_w_out)
    bsz, seq, d = x.shape
    ctx_len = ctx.shape[1]
    depth = ada_w_down.shape[0]
    assert bsz == 1 and ctx_len == ROW_TILE and seq % (GRID_W * (ROW_TILE // GRID_W)) == 0
    cond = jnp.zeros((8, d), F32).at[0].set(c_ctx).at[1].set(c[0])
    mods = ada_all(cond, ada_w_down, ada_w_up, ada_b)[:, :2].reshape(depth, 2, 6, 1, d)
    h = jnp.concatenate([ctx[0], x[0]], axis=0)
    for i in range(depth):
        kind, j = i % 3, i // 3
        sa, ca, ga, sf, cf, gf = [mods[i, :, n] for n in range(6)]
        if kind == 0:
            u = norm_mod(h, norm_g[i, 0], ca, sa, ctx_len, F32)
            h = resid_glu(h, _s5_layer(u, p, j, ctx_len), norm_g[i, 1], ga, ctx_len)
        else:
            u = norm_mod(h, norm_g[i, 0], ca, sa, ctx_len, MXU_DTYPE)
            y = _gla_layer(u, p, j, ctx_len) if kind == 1 else _mlstm_layer(u, p, j, ctx_len)
            h = resid(h, y, norm_g[i, 1], ga, ctx_len)
        uf = norm_mod(h, norm_g[i, 2], cf, sf, ctx_len, MXU_DTYPE)
        f2 = ffn_w_in.shape[2]
        gu = matmul(uf, ffn_w_in[i].astype(MXU_DTYPE), jnp.zeros((f2,), F32), MXU_DTYPE)
        act = conv_act(gu, ffn_conv_w[i], ffn_conv_b[i], ctx_len)
        f = matmul(act, ffn_w_out[i].astype(MXU_DTYPE), jnp.zeros((d,), F32), F32)
        h = resid(h, f, norm_g[i, 3], gf, ctx_len)
    return h[ctx_len:].reshape(bsz, seq, d)
```

```python
import functools
import math

import numpy as np
import jax
import jax.numpy as jnp
from jax import lax
from jax.experimental import pallas as pl
from jax.experimental.pallas import tpu as pltpu

EPS = 1e-6
GRID_W = 64
S5_GROUP = 16
S5_T = 16
S5_GROUP_BLOCK = 4
HEADS = 8
HEAD_BLOCK = 4
CHUNK = 64
GLA_TAU = 16.0
GATE_PAD = 128
ROW_TILE = 256
MXU_DTYPE = jnp.bfloat16
V7X_VMEM_LIMIT_BYTES = 48 * 1024 * 1024

F32 = jnp.float32


def _cparams(*sem):
    return pltpu.CompilerParams(dimension_semantics=sem, vmem_limit_bytes=V7X_VMEM_LIMIT_BYTES)


def _pick(dim, candidates):
    for c in candidates:
        if dim % c == 0:
            return c
    raise ValueError(f"no tile for {dim} in {candidates}")


def _log_sigmoid(x):
    return jnp.minimum(x, 0.0) - jnp.log1p(jnp.exp(-jnp.abs(x)))


def _sigmoid(x):
    return 1.0 / (1.0 + jnp.exp(-x))


def _silu(x):
    return x * _sigmoid(x)


def _gelu_tanh(x):
    return 0.5 * x * (1.0 + jnp.tanh(math.sqrt(2.0 / math.pi) * (x + 0.044715 * (x * x * x))))


def _ada_kernel(cond_ref, wd_ref, wu_ref, b_ref, o_ref):
    s = _silu(cond_ref[...])
    t = jnp.dot(s, wd_ref[0], preferred_element_type=F32, precision=lax.Precision.HIGHEST)
    o_ref[0] = jnp.dot(t, wu_ref[0], preferred_element_type=F32, precision=lax.Precision.HIGHEST) + b_ref[0]


def ada_all(cond, w_down, w_up, b):
    depth, d, rank = w_down.shape
    n = w_up.shape[2]
    tn = _pick(n, (2048, 1024, 512, 256, 128))
    return pl.pallas_call(
        _ada_kernel,
        grid=(depth, n // tn),
        in_specs=[
            pl.BlockSpec((8, d), lambda l, j: (0, 0)),
            pl.BlockSpec((1, d, rank), lambda l, j: (l, 0, 0)),
            pl.BlockSpec((1, rank, tn), lambda l, j: (l, 0, j)),
            pl.BlockSpec((1, 1, tn), lambda l, j: (l, 0, j)),
        ],
        out_specs=pl.BlockSpec((1, 8, tn), lambda l, j: (l, 0, j)),
        out_shape=jax.ShapeDtypeStruct((depth, 8, n), F32),
        compiler_params=_cparams("arbitrary", "arbitrary"),
        name="ada_all",
    )(cond, w_down, w_up, b.reshape(depth, 1, n))


def _rms(x, g):
    return x * lax.rsqrt(jnp.mean(x * x, axis=-1, keepdims=True) + EPS) * g


def _norm_mod_kernel(x_ref, g_ref, sc_ref, sh_ref, o_ref):
    y = _rms(x_ref[...], g_ref[...])
    o_ref[...] = (y * (1.0 + sc_ref[0]) + sh_ref[0]).astype(o_ref.dtype)


def _row_specs(d, ctx_tiles, row_offset=0):
    row = pl.BlockSpec((ROW_TILE, d), lambda i: (i + row_offset, 0))
    vec = pl.BlockSpec((1, d), lambda i: (0, 0))
    mod = pl.BlockSpec((1, 1, d), lambda i: (jnp.where(i + row_offset < ctx_tiles, 0, 1), 0, 0))
    return row, vec, mod


def norm_mod(x, g, scale, shift, ctx_len, out_dtype):
    m, d = x.shape
    row, vec, mod = _row_specs(d, ctx_len // ROW_TILE)
    return pl.pallas_call(
        _norm_mod_kernel,
        grid=(m // ROW_TILE,),
        in_specs=[row, vec, mod, mod],
        out_specs=row,
        out_shape=jax.ShapeDtypeStruct((m, d), out_dtype),
        compiler_params=_cparams("parallel"),
        name="norm_mod",
    )(x, g.reshape(1, d), scale, shift)


def _resid_kernel(*refs, glu, has_next):
    h_ref, y_ref = refs[0], refs[1]
    n_in = 2 + (1 if glu else 0)
    g_ref, gate_ref = refs[n_in], refs[n_in + 1]
    y = y_ref[...].astype(F32)
    if glu:
        y = y * _sigmoid(refs[2][...].astype(F32))
    h = h_ref[...] + gate_ref[0] * _rms(y, g_ref[...])
    if has_next:
        gn_ref, sc_ref, sh_ref, o_ref, u_ref = refs[n_in + 2:]
        u_ref[...] = (_rms(h, gn_ref[...]) * (1.0 + sc_ref[0]) + sh_ref[0]).astype(u_ref.dtype)
    else:
        o_ref = refs[n_in + 2]
    o_ref[...] = h


def resid(h, y, g, gate, ctx_len, *, glu=False, nxt=None, drop_ctx=False):
    m, d = h.shape
    ctx_tiles = ctx_len // ROW_TILE
    off = ctx_tiles if drop_ctx else 0
    row, vec, mod = _row_specs(d, ctx_tiles, off)
    out_row = pl.BlockSpec((ROW_TILE, d), lambda i: (i, 0))
    m_out = m - off * ROW_TILE
    ins, in_specs = [h, y], [row, row]
    if glu:
        ins.append(y)
        in_specs.append(pl.BlockSpec((ROW_TILE, d), lambda i: (i + off, 1)))
    ins += [g.reshape(1, d), gate]
    in_specs += [vec, mod]
    out_shape = [jax.ShapeDtypeStruct((m_out, d), F32)]
    out_specs = [out_row]
    if nxt is not None:
        g_next, scale, shift, dtype = nxt
        ins += [g_next.reshape(1, d), scale, shift]
        in_specs += [vec, mod, mod]
        out_shape.append(jax.ShapeDtypeStruct((m_out, d), dtype))
        out_specs.append(out_row)
    res = pl.pallas_call(
        functools.partial(_resid_kernel, glu=glu, has_next=nxt is not None),
        grid=(m_out // ROW_TILE,),
        in_specs=in_specs,
        out_specs=out_specs,
        out_shape=out_shape,
        compiler_params=_cparams("parallel"),
        name="resid",
    )(*ins)
    return res if nxt is not None else res[0]


def _head_gate_kernel(o_ref, gate_ref, g_ref, out_ref, *, heads, sigmoid_gate):
    o = o_ref[0] + o_ref[1]
    dv = o.shape[1] // heads
    for h in range(heads):
        sl = slice(h * dv, (h + 1) * dv)
        gt = gate_ref[:, sl].astype(F32)
        act = _sigmoid(gt) if sigmoid_gate else _silu(gt)
        out_ref[:, sl] = (_rms(o[:, sl], g_ref[...]) * act).astype(out_ref.dtype)


def head_gate(o2, proj, gate_col_block, norm_g, sigmoid_gate):
    _, m, d = o2.shape
    dv = d // HEADS
    return pl.pallas_call(
        functools.partial(_head_gate_kernel, heads=HEADS, sigmoid_gate=sigmoid_gate),
        grid=(m // ROW_TILE,),
        in_specs=[
            pl.BlockSpec((2, ROW_TILE, d), lambda i: (0, i, 0)),
            pl.BlockSpec((ROW_TILE, d), lambda i: (i, gate_col_block)),
            pl.BlockSpec((1, dv), lambda i: (0, 0)),
        ],
        out_specs=pl.BlockSpec((ROW_TILE, d), lambda i: (i, 0)),
        out_shape=jax.ShapeDtypeStruct((m, d), MXU_DTYPE),
        compiler_params=_cparams("parallel"),
        name="head_gate",
    )(o2, proj, norm_g.reshape(1, dv))


def _mm_kernel(a_ref, w_ref, b_ref, o_ref, acc_ref, *, nk):
    k = pl.program_id(2)

    @pl.when(k == 0)
    def _():
        acc_ref[...] = jnp.zeros_like(acc_ref)

    acc_ref[...] += jnp.dot(a_ref[...].astype(MXU_DTYPE), w_ref[...], preferred_element_type=F32)

    @pl.when(k == nk - 1)
    def _():
        o_ref[...] = (acc_ref[...] + b_ref[...]).astype(o_ref.dtype)


def _mm_full_k_kernel(a_ref, w_ref, b_ref, o_ref):
    acc = jnp.dot(a_ref[...].astype(MXU_DTYPE), w_ref[...], preferred_element_type=F32)
    o_ref[...] = (acc + b_ref[...]).astype(o_ref.dtype)


MM_FULL_K_MAX = 4096


def matmul(a, w, bias, out_dtype):
    m, k = a.shape
    n = w.shape[1]
    tm = _pick(m, (1280, 1024, 640, 512, 256))
    tn = _pick(n, (1024, 512, 256, 128))
    bias2 = bias.reshape(1, n).astype(F32)
    if k <= MM_FULL_K_MAX:
        return pl.pallas_call(
            _mm_full_k_kernel,
            grid=(m // tm, n // tn),
            in_specs=[
                pl.BlockSpec((tm, k), lambda i, j: (i, 0), pipeline_mode=pl.Buffered(1)),
                pl.BlockSpec((k, tn), lambda i, j: (0, j)),
                pl.BlockSpec((1, tn), lambda i, j: (0, j)),
            ],
            out_specs=pl.BlockSpec((tm, tn), lambda i, j: (i, j)),
            out_shape=jax.ShapeDtypeStruct((m, n), out_dtype),
            compiler_params=_cparams("parallel", "arbitrary"),
            name="matmul_full_k",
        )(a, w, bias2)
    tk = _pick(k, (2048, 1024, 512, 256, 128))
    nk = k // tk
    return pl.pallas_call(
        functools.partial(_mm_kernel, nk=nk),
        grid=(m // tm, n // tn, nk),
        in_specs=[
            pl.BlockSpec((tm, tk), lambda i, j, kk: (i, kk)),
            pl.BlockSpec((tk, tn), lambda i, j, kk: (kk, j)),
            pl.BlockSpec((1, tn), lambda i, j, kk: (0, j)),
        ],
        out_specs=pl.BlockSpec((tm, tn), lambda i, j, kk: (i, j)),
        out_shape=jax.ShapeDtypeStruct((m, n), out_dtype),
        scratch_shapes=[pltpu.VMEM((tm, tn), F32)],
        compiler_params=_cparams("parallel", "parallel", "arbitrary"),
        name="matmul",
    )(a, w, bias2)


def _conv_act_kernel(gp_ref, gc_ref, gn_ref, up_ref, w_ref, b_ref, o_ref, *, tb, ctx_tiles, grid_w):
    i = pl.program_id(0)
    n_tiles = pl.num_programs(0)
    up = up_ref[...].astype(F32)

    def finish(conv):
        o_ref[...] = (_silu(conv + b_ref[...]) * up).astype(o_ref.dtype)

    def tap(dr, dc):
        k = 3 * (dr + 1) + (dc + 1)
        return w_ref[k:k + 1, :]

    @pl.when(i < ctx_tiles)
    def _():
        g = gc_ref[...].astype(F32)
        pos = lax.broadcasted_iota(jnp.int32, (tb, 1), 0)
        left = jnp.where(pos > 0, pltpu.roll(g, 1, axis=0), 0.0)
        right = jnp.where(pos < tb - 1, pltpu.roll(g, tb - 1, axis=0), 0.0)
        finish(tap(0, -1) * left + tap(0, 0) * g + tap(0, 1) * right)

    @pl.when(i >= ctx_tiles)
    def _():
        prev = jnp.where(i > ctx_tiles, gp_ref[...].astype(F32), 0.0)
        nxt = jnp.where(i < n_tiles - 1, gn_ref[...].astype(F32), 0.0)
        g = jnp.concatenate([prev, gc_ref[...].astype(F32), nxt], axis=0)
        n_ext = tb + 2 * grid_w
        col = jnp.bitwise_and(lax.broadcasted_iota(jnp.int32, (n_ext, 1), 0), grid_w - 1)
        src = {-1: jnp.where(col > 0, pltpu.roll(g, 1, axis=0), 0.0),
               0: g,
               1: jnp.where(col < grid_w - 1, pltpu.roll(g, n_ext - 1, axis=0), 0.0)}
        conv = None
        for dr in (-1, 0, 1):
            lo = grid_w + dr * grid_w
            for dc in (-1, 0, 1):
                term = tap(dr, dc) * src[dc][lo:lo + tb]
                conv = term if conv is None else conv + term
        finish(conv)


def conv_act(gu, conv_w, conv_b, ctx_len):
    m, f2 = gu.shape
    f = f2 // 2
    tb = ROW_TILE
    assert GRID_W & (GRID_W - 1) == 0 and tb % GRID_W == 0 and ctx_len == tb
    tc = _pick(f, (1024, 512, 256, 128))
    nblk = m // GRID_W
    r = tb // GRID_W
    kern = functools.partial(_conv_act_kernel, tb=tb, ctx_tiles=ctx_len // tb, grid_w=GRID_W)
    return pl.pallas_call(
        kern,
        grid=(m // tb, f // tc),
        in_specs=[
            pl.BlockSpec((GRID_W, tc), lambda i, j: (jnp.maximum(i * r - 1, 0), j)),
            pl.BlockSpec((tb, tc), lambda i, j: (i, j)),
            pl.BlockSpec((GRID_W, tc), lambda i, j: (jnp.minimum((i + 1) * r, nblk - 1), j)),
            pl.BlockSpec((tb, tc), lambda i, j: (i, j + f // tc)),
            pl.BlockSpec((9, tc), lambda i, j: (0, j)),
            pl.BlockSpec((1, tc), lambda i, j: (0, j)),
        ],
        out_specs=pl.BlockSpec((tb, tc), lambda i, j: (i, j)),
        out_shape=jax.ShapeDtypeStruct((m, f), MXU_DTYPE),
        compiler_params=_cparams("parallel", "parallel"),
        name="conv_act",
    )(gu, gu, gu, gu, conv_w.reshape(9, f), conv_b.reshape(1, f))


def _s5_weights(lam_re, lam_im, log_step, b_re, b_im, c_re, c_im):
    t = S5_T
    lam = lax.complex(lam_re.astype(F32), lam_im.astype(F32))
    step = jnp.exp(log_step.astype(F32))[:, :, None]
    pw = jnp.exp(jnp.arange(t + 1, dtype=F32)[:, None, None, None] * (lam * step)[None])
    lam_bar = pw[1]
    b_bar = ((lam_bar - 1.0) / lam)[..., None] * lax.complex(b_re.astype(F32), b_im.astype(F32))
    c_mat = lax.complex(c_re.astype(F32), c_im.astype(F32))
    g_n, p_n, i_n = b_bar.shape[1], b_bar.shape[2], b_bar.shape[3]
    kern = jnp.real(jnp.einsum('dgip,tdgp,dgpj->tdgij', c_mat, pw[:t], b_bar,
                               precision=lax.Precision.HIGHEST))
    s_in = np.arange(t)[:, None]
    s_out = np.arange(t)[None, :]
    lag_f = s_out - s_in
    lag_b = s_in - s_out
    wf = kern[np.clip(lag_f, 0, t - 1), 0] * jnp.asarray(lag_f >= 0, F32)[:, :, None, None, None]
    wb = kern[np.clip(lag_b, 0, t - 1), 1] * jnp.asarray(lag_b >= 0, F32)[:, :, None, None, None]
    w_intra = jnp.transpose(wf + wb, (2, 0, 4, 1, 3)).reshape(g_n, t * i_n, t * i_n)
    vf = pw[t - 1 - np.arange(t), 0][..., None] * b_bar[0][None]
    vb = pw[np.arange(t), 1][..., None] * b_bar[1][None]
    w_in = jnp.concatenate([jnp.real(vf), jnp.real(vb), jnp.imag(vf), jnp.imag(vb)], axis=2)
    w_in = jnp.transpose(w_in, (1, 0, 3, 2)).reshape(g_n, t * i_n, 4 * p_n)
    qf = c_mat[0][None] * pw[1 + np.arange(t), 0][:, :, None, :]
    qb = c_mat[1][None] * pw[t - np.arange(t), 1][:, :, None, :]
    w_out = jnp.concatenate([jnp.real(qf), jnp.real(qb), -jnp.imag(qf), -jnp.imag(qb)], axis=3)
    w_out = jnp.transpose(w_out, (1, 3, 0, 2)).reshape(g_n, 4 * p_n, t * i_n)
    a16 = pw[t]
    a_re = jnp.concatenate([jnp.real(a16[0]), jnp.real(a16[1])], axis=-1)[:, None, :]
    a_im = jnp.concatenate([jnp.imag(a16[0]), jnp.imag(a16[1])], axis=-1)[:, None, :]
    return (w_in.astype(MXU_DTYPE), w_intra.astype(MXU_DTYPE), w_out.astype(MXU_DTYPE), a_re, a_im)


def _lane_block(lane, b, width):
    return jnp.logical_and(lane >= b * width, lane < (b + 1) * width)


def _s5_kernel(*refs, n_rows, n_ctx_rows, p2, t_fold, grp, scan_groups, row_chunk):
    x_refs = refs[:t_fold]
    d_ref, win_ref, wintra_ref, wout_ref, ar_ref, ai_ref, o_ref = refs[t_fold:t_fold + 7]
    ug_ref, zy_ref, hrf_ref, hrb_ref, hif_ref, hib_ref = refs[t_fold + 7:]
    lanes = x_refs[0].shape[1]
    n_grp = lanes // grp
    slots = lanes // grp
    n_half = t_fold // slots
    half = p2 // 2
    lane = lax.broadcasted_iota(jnp.int32, (1, lanes), 1)
    n_chunks = n_rows // row_chunk

    def gather(rc, _):
        rows = pl.ds(pl.multiple_of(rc * row_chunk, 16), row_chunk)
        for g in range(n_grp):
            parts = []
            for hf in range(n_half):
                acc = None
                for s8 in range(slots):
                    xs = x_refs[hf * slots + s8][rows, :]
                    shift = ((s8 - g) * grp) % lanes
                    r = pltpu.roll(xs, shift, axis=1) if shift else xs
                    acc = r if acc is None else jnp.where(_lane_block(lane, s8, grp), r, acc)
                parts.append(acc)
            ug_ref[g, rows, :] = jnp.concatenate(parts, axis=1).astype(ug_ref.dtype)
        return 0

    lax.fori_loop(0, n_chunks, gather, 0)

    is_f = lax.broadcasted_iota(jnp.int32, (1, p2), 1) < half
    sub = 16
    n_tiles = n_rows // sub
    n_ctx_tiles = n_ctx_rows // sub

    for blk in range(n_grp // scan_groups):
        gs = [blk * scan_groups + k for k in range(scan_groups)]
        for g in gs:
            zy_ref[g] = jnp.dot(ug_ref[g], win_ref[g], preferred_element_type=F32)
        a_r = [ar_ref[g] for g in gs]
        a_i = [ai_ref[g] for g in gs]

        def step(i, carry, gs=gs, a_r=a_r, a_i=a_i):
            tf = pl.multiple_of(i * sub, sub)
            tb = pl.multiple_of(
                jnp.where(i < n_ctx_tiles, n_ctx_tiles - 1 - i, n_tiles - 1 - (i - n_ctx_tiles)) * sub, sub)
            out = []
            for k, g in enumerate(gs):
                h_r, h_i = carry[2 * k], carry[2 * k + 1]
                zrf, zrb = zy_ref[g, pl.ds(tf, sub), 0:p2], zy_ref[g, pl.ds(tb, sub), 0:p2]
                zif, zib = zy_ref[g, pl.ds(tf, sub), p2:2 * p2], zy_ref[g, pl.ds(tb, sub), p2:2 * p2]
                rows_r, rows_i = [], []
                for j in range(sub):
                    rows_r.append(h_r)
                    rows_i.append(h_i)
                    z_r = jnp.where(is_f, zrf[j:j + 1], zrb[sub - 1 - j:sub - j])
                    z_i = jnp.where(is_f, zif[j:j + 1], zib[sub - 1 - j:sub - j])
                    h_r, h_i = a_r[k] * h_r - a_i[k] * h_i + z_r, a_r[k] * h_i + a_i[k] * h_r + z_i
                hrf_ref[k, pl.ds(tf, sub), :] = jnp.concatenate(rows_r, axis=0).astype(hrf_ref.dtype)
                hrb_ref[k, pl.ds(tb, sub), :] = jnp.concatenate(rows_r[::-1], axis=0).astype(hrb_ref.dtype)
                hif_ref[k, pl.ds(tf, sub), :] = jnp.concatenate(rows_i, axis=0).astype(hif_ref.dtype)
                hib_ref[k, pl.ds(tb, sub), :] = jnp.concatenate(rows_i[::-1], axis=0).astype(hib_ref.dtype)
                out += [h_r, h_i]
            return tuple(out)

        zero = jnp.zeros((1, p2), F32)
        lax.fori_loop(0, n_tiles, step, tuple(zero for _ in range(2 * scan_groups)))
        for k, g in enumerate(gs):
            h_r = jnp.where(is_f, hrf_ref[k], hrb_ref[k])
            h_i = jnp.where(is_f, hif_ref[k], hib_ref[k])
            hp = jnp.concatenate([h_r, h_i], axis=1)
            zy_ref[g] = (jnp.dot(ug_ref[g], wintra_ref[g], preferred_element_type=F32)
                         + jnp.dot(hp, wout_ref[g], preferred_element_type=F32))

    def scatter(rc, _):
        rows = pl.ds(pl.multiple_of(rc * row_chunk, 16), row_chunk)
        for t in range(t_fold):
            hf, t8 = divmod(t, slots)
            acc = None
            for g in range(n_grp):
                piece = zy_ref[g, rows, hf * lanes:(hf + 1) * lanes]
                shift = ((g - t8) * grp) % lanes
                r = pltpu.roll(piece, shift, axis=1) if shift else piece
                acc = r if acc is None else jnp.where(_lane_block(lane, g, grp), r, acc)
            o_ref[t, rows, :] = _gelu_tanh(d_ref[...] * x_refs[t][rows, :] + acc).astype(o_ref.dtype)
        return 0

    lax.fori_loop(0, n_chunks, scatter, 0)


def s5_scan_act(u, d_skip, weights, ctx_len):
    m, d = u.shape
    w_in, w_intra, w_out, a_re, a_im = weights
    t_fold, grp, lanes = S5_T, S5_GROUP, 128
    n_rows = m // t_fold
    width = t_fold * grp
    p2 = a_re.shape[-1]
    n_grp = lanes // grp
    n_tile = d // lanes
    row_chunk = _pick(n_rows, (80, 16))
    assert n_rows % 16 == 0 and (ctx_len // t_fold) % 16 == 0 and t_fold % n_grp == 0
    kern = functools.partial(_s5_kernel, n_rows=n_rows, n_ctx_rows=ctx_len // t_fold, p2=p2, t_fold=t_fold,
                             grp=grp, scan_groups=S5_GROUP_BLOCK, row_chunk=row_chunk)
    blk = lambda a, b: pl.BlockSpec((n_grp, a, b), lambda i: (i, 0, 0))
    x_specs = [pl.BlockSpec((n_rows, lanes), functools.partial(lambda i, s: (0, s * n_tile + i), s=s),
                            pipeline_mode=pl.Buffered(1)) for s in range(t_fold)]
    u2 = u.reshape(n_rows, t_fold * d)
    out = pl.pallas_call(
        kern,
        grid=(n_tile,),
        in_specs=x_specs + [pl.BlockSpec((1, lanes), lambda i: (0, i)),
                            blk(width, 2 * p2), blk(width, width), blk(2 * p2, width), blk(1, p2), blk(1, p2)],
        out_specs=pl.BlockSpec((t_fold, n_rows, lanes), lambda i: (0, 0, i)),
        out_shape=jax.ShapeDtypeStruct((t_fold, n_rows, d), MXU_DTYPE),
        scratch_shapes=[pltpu.VMEM((n_grp, n_rows, width), MXU_DTYPE), pltpu.VMEM((n_grp, n_rows, 2 * p2), F32)]
        + [pltpu.VMEM((S5_GROUP_BLOCK, n_rows, p2), MXU_DTYPE)] * 4,
        compiler_params=_cparams("parallel"),
        name="s5_scan_act",
    )(*([u2] * t_fold), d_skip.reshape(1, d), w_in, w_intra, w_out, a_re, a_im)
    return jnp.transpose(out, (1, 0, 2)).reshape(m, d)


def _chunk_index(d, c, n_ctx_chunks, n_chunks):
    back = jnp.where(c < n_ctx_chunks, n_ctx_chunks - 1 - c, n_chunks - 1 - (c - n_ctx_chunks))
    return jnp.where(d == 0, c, back)


def _tri_masks():
    lower = np.tril(np.ones((CHUNK, CHUNK), np.float32))
    return jnp.asarray(np.stack([np.stack([lower, lower.T]), np.stack([lower.T, lower])]))


def _split_hi_lo(x):
    hi = x.astype(jnp.bfloat16)
    lo = (x - hi.astype(F32)).astype(jnp.bfloat16)
    return hi, lo


def _gla_kernel(q_ref, k_ref, v_ref, r_ref, wg_ref, bg_ref, tri_ref, o_ref, s_ref, *, dk, dv, hb):
    @pl.when(pl.program_id(2) == 0)
    def _():
        s_ref[...] = jnp.zeros_like(s_ref)

    w = hb * dk
    x = jnp.dot(r_ref[...].astype(MXU_DTYPE), wg_ref[0].astype(MXU_DTYPE), preferred_element_type=F32) + bg_ref[0]
    la = _log_sigmoid(x) * (1.0 / GLA_TAU)
    tri = tri_ref[0, 0]
    vis = tri > 0
    hi, lo = _split_hi_lo(la)
    bb = jnp.dot(tri.astype(jnp.bfloat16), jnp.concatenate([hi, lo], axis=1), preferred_element_type=F32)
    b = bb[:, :w] + bb[:, w:]
    tot = jnp.sum(la, axis=0, keepdims=True)
    mid = 0.5 * tot
    q = q_ref[...].astype(F32) * dk ** -0.5
    k = k_ref[...].astype(F32)
    qe = (q * jnp.exp(b - mid)).astype(MXU_DTYPE)
    ke = (k * jnp.exp(mid - b)).astype(MXU_DTYPE)
    qb = (q * jnp.exp(b)).astype(MXU_DTYPE)
    kd = (k * jnp.exp(tot - b)).astype(MXU_DTYPE)
    dec = jnp.exp(tot)
    for h in range(hb):
        ks = slice(h * dk, (h + 1) * dk)
        vs = slice(h * dv, (h + 1) * dv)
        v = v_ref[:, vs].astype(MXU_DTYPE)
        att = lax.dot_general(qe[:, ks], ke[:, ks], (((1,), (1,)), ((), ())), preferred_element_type=F32)
        att = jnp.where(vis, att, 0.0).astype(MXU_DTYPE)
        st = s_ref[h]
        o_ref[0, :, vs] = (jnp.dot(att, v, preferred_element_type=F32)
                           + lax.dot_general(qb[:, ks], st.astype(MXU_DTYPE), (((1,), (1,)), ((), ())),
                                             preferred_element_type=F32))
        upd = lax.dot_general(v, kd[:, ks], (((0,), (0,)), ((), ())), preferred_element_type=F32)
        s_ref[h] = st * dec[:, ks] + upd


def gla_scan(proj, r, w_gate_up, b_gate, ctx_len):
    m = proj.shape[0]
    n_dir, rank, qk = w_gate_up.shape
    dk = qk // HEADS
    dv = 2 * dk
    hb = HEAD_BLOCK
    nhb = HEADS // hb
    n_chunks = m // CHUNK
    n_ctx = ctx_len // CHUNK
    wg = jnp.zeros((n_dir, GATE_PAD, qk), F32)
    for dr in range(n_dir):
        wg = wg.at[dr, dr * rank:(dr + 1) * rank].set(w_gate_up[dr])
    cidx = functools.partial(_chunk_index, n_ctx_chunks=n_ctx, n_chunks=n_chunks)
    return pl.pallas_call(
        functools.partial(_gla_kernel, dk=dk, dv=dv, hb=hb),
        grid=(n_dir, nhb, n_chunks),
        in_specs=[
            pl.BlockSpec((CHUNK, hb * dk), lambda d, h, c: (cidx(d, c), h)),
            pl.BlockSpec((CHUNK, hb * dk), lambda d, h, c: (cidx(d, c), nhb + h)),
            pl.BlockSpec((CHUNK, hb * dv), lambda d, h, c: (cidx(d, c), nhb + h)),
            pl.BlockSpec((CHUNK, GATE_PAD), lambda d, h, c: (cidx(d, c), 0)),
            pl.BlockSpec((1, GATE_PAD, hb * dk), lambda d, h, c: (d, 0, h)),
            pl.BlockSpec((1, 1, hb * dk), lambda d, h, c: (d, 0, h)),
            pl.BlockSpec((1, 1, CHUNK, CHUNK), lambda d, h, c: (d, 0, 0, 0)),
        ],
        out_specs=pl.BlockSpec((1, CHUNK, hb * dv), lambda d, h, c: (d, cidx(d, c), h)),
        out_shape=jax.ShapeDtypeStruct((n_dir, m, dv * HEADS), F32),
        scratch_shapes=[pltpu.VMEM((hb, dv, dk), F32)],
        compiler_params=_cparams("arbitrary", "arbitrary", "arbitrary"),
        name="gla_scan",
    )(proj, proj, proj, r, wg, b_gate.reshape(n_dir, 1, qk), _tri_masks())


def _mlstm_kernel(q_ref, k_ref, v_ref, gcol_ref, grow_ref, tri_ref, o_ref, c_ref, m_ref, *, dk, dv, hb):
    d = pl.program_id(0)
    h0 = pl.program_id(1) * hb

    @pl.when(pl.program_id(2) == 0)
    def _():
        c_ref[...] = jnp.zeros_like(c_ref)
        m_ref[...] = jnp.zeros_like(m_ref)

    t = q_ref.shape[0]
    tri = tri_ref[0, 0]
    tri_t = tri_ref[0, 1]
    vis = tri > 0
    lane = lax.broadcasted_iota(jnp.int32, (1, gcol_ref.shape[1]), 1)
    gcol = gcol_ref[...]
    row = lax.broadcasted_iota(jnp.int32, (t, 1), 0)
    last = jnp.where(d == 0, t - 1, 0)
    e0 = (lax.broadcasted_iota(jnp.int32, (t, 128), 1) == 0).astype(MXU_DTYPE)
    for h in range(hb):
        ks = slice(h * dk, (h + 1) * dk)
        vs = slice(h * dv, (h + 1) * dv)
        col_i = 2 * d * HEADS + h0 + h
        li_c = jnp.sum(jnp.where(lane == col_i, gcol, 0.0), axis=1, keepdims=True)
        lf_c = _log_sigmoid(jnp.sum(jnp.where(lane == col_i + HEADS, gcol, 0.0), axis=1, keepdims=True))
        li_r = grow_ref[0, pl.ds(col_i, 1), :]
        lf_r = _log_sigmoid(grow_ref[0, pl.ds(col_i + HEADS, 1), :])
        b_c = jnp.sum(tri * lf_r, axis=1, keepdims=True)
        b_r = jnp.sum(tri_t * lf_c, axis=0, keepdims=True)
        b_last = jnp.sum(lf_r, axis=1, keepdims=True)
        m0 = m_ref[h][:, 0:1]
        logw = jnp.where(vis, b_c - b_r + li_r, -jnp.inf)
        w_inter = b_c + m0
        m_t = jnp.maximum(w_inter, jnp.max(logw, axis=1, keepdims=True))
        m_new = jnp.max(jnp.where(row == last, m_t, -jnp.inf), axis=0, keepdims=True)
        carry = jnp.exp(b_last + m0 - m_new)
        w_last_c = jnp.exp(b_last - b_c + li_c - m_new)
        q = q_ref[:, ks].astype(MXU_DTYPE)
        kf = k_ref[:, ks].astype(F32) * dk ** -0.5
        v_ext = jnp.concatenate([v_ref[:, vs].astype(MXU_DTYPE), e0], axis=1)
        c0 = c_ref[h]
        qk = lax.dot_general(q, kf.astype(MXU_DTYPE), (((1,), (1,)), ((), ())), preferred_element_type=F32)
        qk = qk * jnp.exp(logw - m_t)
        inter = jnp.exp(w_inter - m_t)
        num = (jnp.dot(qk.astype(MXU_DTYPE), v_ext, preferred_element_type=F32)
               + inter * jnp.dot(q, c0.astype(MXU_DTYPE), preferred_element_type=F32))
        den = num[:, dv:dv + 1]
        o_ref[0, :, vs] = num[:, :dv] / jnp.maximum(jnp.abs(den), jnp.exp(-m_t))
        kw = (kf * w_last_c).astype(MXU_DTYPE)
        c_ref[h] = carry * c0 + lax.dot_general(kw, v_ext, (((0,), (0,)), ((), ())), preferred_element_type=F32)
        m_ref[h] = jnp.broadcast_to(m_new, m_ref.shape[1:])


def mlstm_scan(proj, gates, ctx_len):
    m = proj.shape[0]
    dk = proj.shape[1] // (6 * HEADS)
    dv = 2 * dk
    hb = HEAD_BLOCK
    nhb = HEADS // hb
    n_chunks = m // CHUNK
    n_ctx = ctx_len // CHUNK
    n_gate = 4 * HEADS
    grow = jnp.transpose(gates[:, :n_gate].reshape(n_chunks, CHUNK, n_gate), (0, 2, 1))
    cidx = functools.partial(_chunk_index, n_ctx_chunks=n_ctx, n_chunks=n_chunks)
    return pl.pallas_call(
        functools.partial(_mlstm_kernel, dk=dk, dv=dv, hb=hb),
        grid=(2, nhb, n_chunks),
        in_specs=[
            pl.BlockSpec((CHUNK, hb * dk), lambda d, h, c: (cidx(d, c), h)),
            pl.BlockSpec((CHUNK, hb * dk), lambda d, h, c: (cidx(d, c), nhb + h)),
            pl.BlockSpec((CHUNK, hb * dv), lambda d, h, c: (cidx(d, c), nhb + h)),
            pl.BlockSpec((CHUNK, GATE_PAD), lambda d, h, c: (cidx(d, c), 0)),
            pl.BlockSpec((1, n_gate, CHUNK), lambda d, h, c: (cidx(d, c), 0, 0)),
            pl.BlockSpec((1, 2, CHUNK, CHUNK), lambda d, h, c: (d, 0, 0, 0)),
        ],
        out_specs=pl.BlockSpec((1, CHUNK, hb * dv), lambda d, h, c: (d, cidx(d, c), h)),
        out_shape=jax.ShapeDtypeStruct((2, m, dv * HEADS), F32),
        scratch_shapes=[pltpu.VMEM((hb, dk, dv + 128), F32), pltpu.VMEM((hb, 1, 128), F32)],
        compiler_params=_cparams("arbitrary", "arbitrary", "arbitrary"),
        name="mlstm_scan",
    )(proj, proj, proj, gates, grow, _tri_masks())


def _pad_cols(w, width):
    return jnp.pad(w, ((0, 0), (0, width - w.shape[1])))


def _s5_layer(u, p, j, ctx_len):
    weights = _s5_weights(p['s5_lam_re'][j], p['s5_lam_im'][j], p['s5_log_step'][j], p['s5_b_re'][j],
                          p['s5_b_im'][j], p['s5_c_re'][j], p['s5_c_im'][j])
    act = s5_scan_act(u, p['s5_d'][j], weights, ctx_len)
    return matmul(act, p['s5_w_glu'][j].astype(MXU_DTYPE), p['s5_b_glu'][j], F32)


def _gla_layer(u, p, j, ctx_len):
    w_in = p['gla_w_in'][j]
    n_main = w_in.shape[1] - 2 * p['gla_w_gate_up'].shape[2]
    proj = matmul(u, w_in[:, :n_main].astype(MXU_DTYPE), jnp.zeros((n_main,), F32), MXU_DTYPE)
    r = matmul(u, _pad_cols(w_in[:, n_main:], GATE_PAD).astype(MXU_DTYPE), jnp.zeros((GATE_PAD,), F32), F32)
    o2 = gla_scan(proj, r, p['gla_w_gate_up'][j], p['gla_b_gate'][j], ctx_len)
    on = head_gate(o2, proj, 2, p['gla_norm_g'][j], sigmoid_gate=False)
    d = u.shape[1]
    return matmul(on, p['gla_w_out'][j].astype(MXU_DTYPE), jnp.zeros((d,), F32), F32)


def _mlstm_layer(u, p, j, ctx_len):
    w_in = p['mlstm_w_in'][j]
    n_gate = 4 * HEADS
    n_main = w_in.shape[1] - n_gate
    proj = matmul(u, w_in[:, :n_main].astype(MXU_DTYPE), jnp.zeros((n_main,), F32), MXU_DTYPE)
    b_gate = jnp.pad(p['mlstm_b_gate'][j].reshape(n_gate), (0, GATE_PAD - n_gate))
    gates = matmul(u, _pad_cols(w_in[:, n_main:], GATE_PAD).astype(MXU_DTYPE), b_gate, F32)
    h2 = mlstm_scan(proj, gates, ctx_len)
    hn = head_gate(h2, proj, 2, p['mlstm_norm_g'][j], sigmoid_gate=True)
    d = u.shape[1]
    return matmul(hn, p['mlstm_w_out'][j].astype(MXU_DTYPE), jnp.zeros((d,), F32), F32)


def kernel(x, c, ctx, c_ctx, ada_w_down, ada_w_up, ada_b, norm_g, ffn_w_in, ffn_conv_w, ffn_conv_b, ffn_w_out, s5_lam_re, s5_lam_im, s5_log_step, s5_b_re, s5_b_im, s5_c_re, s5_c_im, s5_d, s5_w_glu, s5_b_glu, gla_w_in, gla_w_gate_up, gla_b_gate, gla_norm_g, gla_w_out, mlstm_w_in, mlstm_b_gate, mlstm_norm_g, mlstm_w_out):
    p = dict(s5_lam_re=s5_lam_re, s5_lam_im=s5_lam_im, s5_log_step=s5_log_step, s5_b_re=s5_b_re, s5_b_im=s5_b_im,
             s5_c_re=s5_c_re, s5_c_im=s5_c_im, s5_d=s5_d, s5_w_glu=s5_w_glu, s5_b_glu=s5_b_glu,
             gla_w_in=gla_w_in, gla_w_gate_up=gla_w_gate_up, gla_b_gate=gla_b_gate, gla_norm_g=gla_norm_g,
             gla_w_out=gla_w_out, mlstm_w_in=mlstm_w_in, mlstm_b_gate=mlstm_b_gate, mlstm_norm_g=mlstm_norm_g,
             mlstm_w_out=mlstm_w_out)
    bsz, seq, d = x.shape
    ctx_len = ctx.shape[1]
    depth = ada_w_down.shape[0]
    assert bsz == 1 and ctx_len == ROW_TILE and seq % (GRID_W * (ROW_TILE // GRID_W)) == 0
    cond = jnp.zeros((8, d), F32).at[0].set(c_ctx).at[1].set(c[0])
    mods = ada_all(cond, ada_w_down, ada_w_up, ada_b)[:, :2].reshape(depth, 2, 6, 1, d)
    mod = lambda i, n: mods[i, :, n]
    mixer_dtype = lambda i: F32 if i % 3 == 0 else MXU_DTYPE
    h = jnp.concatenate([ctx[0], x[0]], axis=0)
    u = norm_mod(h, norm_g[0, 0], mod(0, 1), mod(0, 0), ctx_len, mixer_dtype(0))
    f2 = ffn_w_in.shape[2]
    for i in range(depth):
        kind, j = i % 3, i // 3
        ffn_in = (norm_g[i, 2], mod(i, 4), mod(i, 3), MXU_DTYPE)
        if kind == 0:
            h, uf = resid(h, _s5_layer(u, p, j, ctx_len), norm_g[i, 1], mod(i, 2), ctx_len, glu=True, nxt=ffn_in)
        else:
            y = _gla_layer(u, p, j, ctx_len) if kind == 1 else _mlstm_layer(u, p, j, ctx_len)
            h, uf = resid(h, y, norm_g[i, 1], mod(i, 2), ctx_len, nxt=ffn_in)
        gu = matmul(uf, ffn_w_in[i].astype(MXU_DTYPE), jnp.zeros((f2,), F32), MXU_DTYPE)
        act = conv_act(gu, ffn_conv_w[i], ffn_conv_b[i], ctx_len)
        f = matmul(act, ffn_w_out[i].astype(MXU_DTYPE), jnp.zeros((d,), F32), F32)
        if i + 1 < depth:
            nxt = (norm_g[i + 1, 0], mod(i + 1, 1), mod(i + 1, 0), mixer_dtype(i + 1))
            h, u = resid(h, f, norm_g[i, 3], mod(i, 5), ctx_len, nxt=nxt)
        else:
            h = resid(h, f, norm_g[i, 3], mod(i, 5), ctx_len, drop_ctx=True)
    return h.reshape(bsz, seq, d)
```

```python
import functools
import math

import numpy as np
import jax
import jax.numpy as jnp
from jax import lax
from jax.experimental import pallas as pl
from jax.experimental.pallas import tpu as pltpu

EPS = 1e-6
GRID_W = 64
S5_GROUP = 16
S5_T = 16
S5_GROUP_BLOCK = 4
HEADS = 8
HEAD_BLOCK = 4
CHUNK = 64
GLA_TAU = 16.0
GATE_PAD = 128
ROW_TILE = 256
SUBLANES = 8
MXU_DTYPE = jnp.bfloat16
V7X_VMEM_LIMIT_BYTES = 48 * 1024 * 1024

F32 = jnp.float32


def _cparams(*sem):
    return pltpu.CompilerParams(dimension_semantics=sem, vmem_limit_bytes=V7X_VMEM_LIMIT_BYTES)


def _pick(dim, candidates):
    for c in candidates:
        if dim % c == 0:
            return c
    raise ValueError(f"no tile for {dim} in {candidates}")


def _log_sigmoid(x):
    return jnp.minimum(x, 0.0) - jnp.log1p(jnp.exp(-jnp.abs(x)))


def _sigmoid(x):
    return 1.0 / (1.0 + jnp.exp(-x))


def _silu(x):
    return x * _sigmoid(x)


def _gelu_tanh(x):
    return 0.5 * x * (1.0 + jnp.tanh(math.sqrt(2.0 / math.pi) * (x + 0.044715 * (x * x * x))))


def _ada_kernel(cond_ref, wd_ref, wu_ref, b_ref, o_ref):
    s = _silu(cond_ref[...])
    t = jnp.dot(s, wd_ref[0], preferred_element_type=F32, precision=lax.Precision.HIGHEST)
    o_ref[0] = jnp.dot(t, wu_ref[0], preferred_element_type=F32, precision=lax.Precision.HIGHEST) + b_ref[0]


def ada_all(cond, w_down, w_up, b):
    depth, d, rank = w_down.shape
    n = w_up.shape[2]
    tn = _pick(n, (2048, 1024, 512, 256, 128))
    return pl.pallas_call(
        _ada_kernel,
        grid=(depth, n // tn),
        in_specs=[
            pl.BlockSpec((8, d), lambda l, j: (0, 0)),
            pl.BlockSpec((1, d, rank), lambda l, j: (l, 0, 0)),
            pl.BlockSpec((1, rank, tn), lambda l, j: (l, 0, j)),
            pl.BlockSpec((1, 1, tn), lambda l, j: (l, 0, j)),
        ],
        out_specs=pl.BlockSpec((1, 8, tn), lambda l, j: (l, 0, j)),
        out_shape=jax.ShapeDtypeStruct((depth, 8, n), F32),
        compiler_params=_cparams("arbitrary", "arbitrary"),
        name="ada_all",
    )(cond, w_down, w_up, b.reshape(depth, 1, n))


def _rms(x, g):
    return x * lax.rsqrt(jnp.mean(x * x, axis=-1, keepdims=True) + EPS) * g


def _norm_mod_kernel(x_ref, g_ref, sc_ref, sh_ref, o_ref):
    y = _rms(x_ref[...], g_ref[...])
    o_ref[...] = (y * (1.0 + sc_ref[0]) + sh_ref[0]).astype(o_ref.dtype)


def _row_specs(d, ctx_tiles, row_offset=0):
    row = pl.BlockSpec((ROW_TILE, d), lambda i: (i + row_offset, 0))
    vec = pl.BlockSpec((1, d), lambda i: (0, 0))
    mod = pl.BlockSpec((1, 1, d), lambda i: (jnp.where(i + row_offset < ctx_tiles, 0, 1), 0, 0))
    return row, vec, mod


def norm_mod(x, g, scale, shift, ctx_len, out_dtype):
    m, d = x.shape
    row, vec, mod = _row_specs(d, ctx_len // ROW_TILE)
    return pl.pallas_call(
        _norm_mod_kernel,
        grid=(m // ROW_TILE,),
        in_specs=[row, vec, mod, mod],
        out_specs=row,
        out_shape=jax.ShapeDtypeStruct((m, d), out_dtype),
        compiler_params=_cparams("parallel"),
        name="norm_mod",
    )(x, g.reshape(1, d), scale, shift)


def _resid_kernel(*refs, glu, has_next):
    h_ref, y_ref = refs[0], refs[1]
    n_in = 2 + (1 if glu else 0)
    g_ref, gate_ref = refs[n_in], refs[n_in + 1]
    y = y_ref[...].astype(F32)
    if glu:
        y = y * _sigmoid(refs[2][...].astype(F32))
    h = h_ref[...] + gate_ref[0] * _rms(y, g_ref[...])
    if has_next:
        gn_ref, sc_ref, sh_ref, o_ref, u_ref = refs[n_in + 2:]
        u_ref[...] = (_rms(h, gn_ref[...]) * (1.0 + sc_ref[0]) + sh_ref[0]).astype(u_ref.dtype)
    else:
        o_ref = refs[n_in + 2]
    o_ref[...] = h


def resid(h, y, g, gate, ctx_len, *, glu=False, nxt=None, drop_ctx=False):
    m, d = h.shape
    ctx_tiles = ctx_len // ROW_TILE
    off = ctx_tiles if drop_ctx else 0
    row, vec, mod = _row_specs(d, ctx_tiles, off)
    out_row = pl.BlockSpec((ROW_TILE, d), lambda i: (i, 0))
    m_out = m - off * ROW_TILE
    ins, in_specs = [h, y], [row, row]
    if glu:
        ins.append(y)
        in_specs.append(pl.BlockSpec((ROW_TILE, d), lambda i: (i + off, 1)))
    ins += [g.reshape(1, d), gate]
    in_specs += [vec, mod]
    out_shape = [jax.ShapeDtypeStruct((m_out, d), F32)]
    out_specs = [out_row]
    if nxt is not None:
        g_next, scale, shift, dtype = nxt
        ins += [g_next.reshape(1, d), scale, shift]
        in_specs += [vec, mod, mod]
        out_shape.append(jax.ShapeDtypeStruct((m_out, d), dtype))
        out_specs.append(out_row)
    res = pl.pallas_call(
        functools.partial(_resid_kernel, glu=glu, has_next=nxt is not None),
        grid=(m_out // ROW_TILE,),
        in_specs=in_specs,
        out_specs=out_specs,
        out_shape=out_shape,
        compiler_params=_cparams("parallel"),
        name="resid",
    )(*ins)
    return res if nxt is not None else res[0]


def _head_gate_kernel(o_ref, gate_ref, g_ref, out_ref, *, heads, sigmoid_gate):
    o = o_ref[0] + o_ref[1]
    dv = o.shape[1] // heads
    for h in range(heads):
        sl = slice(h * dv, (h + 1) * dv)
        gt = gate_ref[:, sl].astype(F32)
        act = _sigmoid(gt) if sigmoid_gate else _silu(gt)
        out_ref[:, sl] = (_rms(o[:, sl], g_ref[...]) * act).astype(out_ref.dtype)


def head_gate(o2, proj, gate_col_block, norm_g, sigmoid_gate):
    _, m, d = o2.shape
    dv = d // HEADS
    return pl.pallas_call(
        functools.partial(_head_gate_kernel, heads=HEADS, sigmoid_gate=sigmoid_gate),
        grid=(m // ROW_TILE,),
        in_specs=[
            pl.BlockSpec((2, ROW_TILE, d), lambda i: (0, i, 0)),
            pl.BlockSpec((ROW_TILE, d), lambda i: (i, gate_col_block)),
            pl.BlockSpec((1, dv), lambda i: (0, 0)),
        ],
        out_specs=pl.BlockSpec((ROW_TILE, d), lambda i: (i, 0)),
        out_shape=jax.ShapeDtypeStruct((m, d), MXU_DTYPE),
        compiler_params=_cparams("parallel"),
        name="head_gate",
    )(o2, proj, norm_g.reshape(1, dv))


def _mm_kernel(a_ref, w_ref, b_ref, o_ref, acc_ref, *, nk):
    k = pl.program_id(2)

    @pl.when(k == 0)
    def _():
        acc_ref[...] = jnp.zeros_like(acc_ref)

    acc_ref[...] += jnp.dot(a_ref[...].astype(MXU_DTYPE), w_ref[...], preferred_element_type=F32)

    @pl.when(k == nk - 1)
    def _():
        o_ref[...] = (acc_ref[...] + b_ref[...]).astype(o_ref.dtype)


def _mm_full_k_kernel(a_ref, w_ref, b_ref, o_ref):
    acc = jnp.dot(a_ref[...].astype(MXU_DTYPE), w_ref[...], preferred_element_type=F32)
    o_ref[...] = (acc + b_ref[...]).astype(o_ref.dtype)


MM_FULL_K_MAX = 4096


def matmul(a, w, bias, out_dtype):
    m, k = a.shape
    n = w.shape[1]
    tm = _pick(m, (1280, 1024, 640, 512, 256))
    tn = _pick(n, (1024, 512, 256, 128))
    bias2 = bias.reshape(1, n).astype(F32)
    if k <= MM_FULL_K_MAX:
        return pl.pallas_call(
            _mm_full_k_kernel,
            grid=(m // tm, n // tn),
            in_specs=[
                pl.BlockSpec((tm, k), lambda i, j: (i, 0), pipeline_mode=pl.Buffered(1)),
                pl.BlockSpec((k, tn), lambda i, j: (0, j)),
                pl.BlockSpec((1, tn), lambda i, j: (0, j)),
            ],
            out_specs=pl.BlockSpec((tm, tn), lambda i, j: (i, j)),
            out_shape=jax.ShapeDtypeStruct((m, n), out_dtype),
            compiler_params=_cparams("parallel", "arbitrary"),
            name="matmul_full_k",
        )(a, w, bias2)
    tk = _pick(k, (2048, 1024, 512, 256, 128))
    nk = k // tk
    return pl.pallas_call(
        functools.partial(_mm_kernel, nk=nk),
        grid=(m // tm, n // tn, nk),
        in_specs=[
            pl.BlockSpec((tm, tk), lambda i, j, kk: (i, kk)),
            pl.BlockSpec((tk, tn), lambda i, j, kk: (kk, j)),
            pl.BlockSpec((1, tn), lambda i, j, kk: (0, j)),
        ],
        out_specs=pl.BlockSpec((tm, tn), lambda i, j, kk: (i, j)),
        out_shape=jax.ShapeDtypeStruct((m, n), out_dtype),
        scratch_shapes=[pltpu.VMEM((tm, tn), F32)],
        compiler_params=_cparams("parallel", "parallel", "arbitrary"),
        name="matmul",
    )(a, w, bias2)


def _conv_act_kernel(gp_ref, gc_ref, gn_ref, up_ref, w_ref, b_ref, sh_ref, o_ref, *, tb, ctx_tiles, grid_w):
    i = pl.program_id(0)
    n_tiles = pl.num_programs(0)
    up = up_ref[...].astype(F32)

    def finish(conv):
        o_ref[...] = (_silu(conv + b_ref[...]) * up).astype(o_ref.dtype)

    def tap(dr, dc):
        k = 3 * (dr + 1) + (dc + 1)
        return w_ref[k:k + 1, :]

    @pl.when(i < ctx_tiles)
    def _():
        g = gc_ref[...].astype(F32)
        pos = lax.broadcasted_iota(jnp.int32, (tb, 1), 0)
        left = jnp.where(pos > 0, pltpu.roll(g, 1, axis=0), 0.0)
        right = jnp.where(pos < tb - 1, pltpu.roll(g, tb - 1, axis=0), 0.0)
        finish(tap(0, -1) * left + tap(0, 0) * g + tap(0, 1) * right)

    @pl.when(i >= ctx_tiles)
    def _():
        zero = jnp.zeros_like(gp_ref[...])
        blocks = ([jnp.where(i > ctx_tiles, gp_ref[...], zero)]
                  + [gc_ref[r * grid_w:(r + 1) * grid_w, :] for r in range(tb // grid_w)]
                  + [jnp.where(i < n_tiles - 1, gn_ref[...], zero)])
        shifts = sh_ref[...]
        src = []
        for blk in blocks:
            lr = jnp.dot(shifts, blk, preferred_element_type=F32)
            src.append({-1: lr[:grid_w], 0: blk.astype(F32), 1: lr[grid_w:]})
        rows = []
        for r in range(tb // grid_w):
            conv = None
            for dr in (-1, 0, 1):
                for dc in (-1, 0, 1):
                    term = tap(dr, dc) * src[r + 1 + dr][dc]
                    conv = term if conv is None else conv + term
            rows.append(conv)
        finish(jnp.concatenate(rows, axis=0))


def _shift_mats(n, dtype):
    left = np.eye(n, k=-1, dtype=np.float32)
    return jnp.asarray(np.concatenate([left, left.T], axis=0), dtype)


def conv_act(gu, conv_w, conv_b, ctx_len):
    m, f2 = gu.shape
    f = f2 // 2
    tb = ROW_TILE
    assert GRID_W & (GRID_W - 1) == 0 and tb % GRID_W == 0 and ctx_len == tb
    tc = _pick(f, (1024, 512, 256, 128))
    nblk = m // GRID_W
    r = tb // GRID_W
    kern = functools.partial(_conv_act_kernel, tb=tb, ctx_tiles=ctx_len // tb, grid_w=GRID_W)
    return pl.pallas_call(
        kern,
        grid=(m // tb, f // tc),
        in_specs=[
            pl.BlockSpec((GRID_W, tc), lambda i, j: (jnp.maximum(i * r - 1, 0), j)),
            pl.BlockSpec((tb, tc), lambda i, j: (i, j)),
            pl.BlockSpec((GRID_W, tc), lambda i, j: (jnp.minimum((i + 1) * r, nblk - 1), j)),
            pl.BlockSpec((tb, tc), lambda i, j: (i, j + f // tc)),
            pl.BlockSpec((9, tc), lambda i, j: (0, j)),
            pl.BlockSpec((1, tc), lambda i, j: (0, j)),
            pl.BlockSpec((2 * GRID_W, GRID_W), lambda i, j: (0, 0)),
        ],
        out_specs=pl.BlockSpec((tb, tc), lambda i, j: (i, j)),
        out_shape=jax.ShapeDtypeStruct((m, f), MXU_DTYPE),
        compiler_params=_cparams("parallel", "parallel"),
        name="conv_act",
    )(gu, gu, gu, gu, conv_w.reshape(9, f), conv_b.reshape(1, f), _shift_mats(GRID_W, gu.dtype))


def _s5_weights(lam_re, lam_im, log_step, b_re, b_im, c_re, c_im):
    t = S5_T
    lam = lax.complex(lam_re.astype(F32), lam_im.astype(F32))
    step = jnp.exp(log_step.astype(F32))[:, :, None]
    pw = jnp.exp(jnp.arange(t + 1, dtype=F32)[:, None, None, None] * (lam * step)[None])
    lam_bar = pw[1]
    b_bar = ((lam_bar - 1.0) / lam)[..., None] * lax.complex(b_re.astype(F32), b_im.astype(F32))
    c_mat = lax.complex(c_re.astype(F32), c_im.astype(F32))
    g_n, p_n, i_n = b_bar.shape[1], b_bar.shape[2], b_bar.shape[3]
    kern = jnp.real(jnp.einsum('dgip,tdgp,dgpj->tdgij', c_mat, pw[:t], b_bar,
                               precision=lax.Precision.HIGHEST))
    kf = jnp.transpose(kern[:, 0], (1, 3, 0, 2))
    kb = jnp.transpose(kern[::-1, 1], (1, 3, 0, 2))
    zero = jnp.zeros_like(kf[:, :, :1])
    lags = jnp.concatenate([zero, kb, zero.repeat(t - 1, axis=2)], axis=2) \
        + jnp.concatenate([zero.repeat(t, axis=2), kf], axis=2)
    period = jnp.concatenate([lags, zero], axis=2).reshape(g_n, i_n, (2 * t + 1) * i_n)
    skew = jnp.tile(period, (1, 1, t))[:, :, :t * 2 * t * i_n].reshape(g_n, i_n, t, 2 * t, i_n)
    w_intra = jnp.transpose(skew[:, :, :, t:], (0, 2, 1, 3, 4)).reshape(g_n, t * i_n, t * i_n)
    vf = pw[t - 1 - np.arange(t), 0][..., None] * b_bar[0][None]
    vb = pw[np.arange(t), 1][..., None] * b_bar[1][None]
    w_in = jnp.concatenate([jnp.real(vf), jnp.real(vb), jnp.imag(vf), jnp.imag(vb)], axis=2)
    w_in = jnp.transpose(w_in, (1, 0, 3, 2)).reshape(g_n, t * i_n, 4 * p_n)
    qf = c_mat[0][None] * pw[1 + np.arange(t), 0][:, :, None, :]
    qb = c_mat[1][None] * pw[t - np.arange(t), 1][:, :, None, :]
    w_out = jnp.concatenate([jnp.real(qf), jnp.real(qb), -jnp.imag(qf), -jnp.imag(qb)], axis=3)
    w_out = jnp.transpose(w_out, (1, 3, 0, 2)).reshape(g_n, 4 * p_n, t * i_n)
    sl = SUBLANES
    apow = jnp.exp(((1 + jnp.arange(sl, dtype=F32)) * t)[:, None, None, None] * (lam * step)[None])
    rows = np.arange(sl)
    both = lambda f, b: jnp.concatenate([f, b], axis=-1)
    planes = []
    for dist in (1, 2, 4):
        mf = apow[dist - 1, 0][:, None, :] * jnp.asarray(rows >= dist, F32)[None, :, None]
        mb = apow[dist - 1, 1][:, None, :] * jnp.asarray(rows <= sl - 1 - dist, F32)[None, :, None]
        planes.append(both(mf, mb))
    planes.append(both(jnp.transpose(apow[rows, 0], (1, 0, 2)), jnp.transpose(apow[sl - 1 - rows, 1], (1, 0, 2))))
    scan_mul = jnp.stack([f(pl_) for pl_ in planes for f in (jnp.real, jnp.imag)], axis=1)
    return (w_in.astype(MXU_DTYPE), w_intra.astype(MXU_DTYPE), w_out.astype(MXU_DTYPE), scan_mul)


def _block_transpose(arrs, lane, width):
    n = len(arrs)
    lanes = n * width
    dist = n // 2
    while dist >= 1:
        w = dist * width
        low = jnp.bitwise_and(lane, 2 * w - 1) < w
        nxt = list(arrs)
        for a in range(n):
            if (a // dist) % 2 == 0:
                b = a + dist
                nxt[a] = jnp.where(low, arrs[a], pltpu.roll(arrs[b], w, axis=1))
                nxt[b] = jnp.where(low, pltpu.roll(arrs[a], lanes - w, axis=1), arrs[b])
        arrs = nxt
        dist //= 2
    return arrs


def _s5_kernel(*refs, n_rows, n_ctx_rows, p2, t_fold, grp, scan_groups, row_chunk):
    x_refs = refs[:t_fold]
    d_ref, win_ref, wintra_ref, wout_ref, mul_ref, o_ref = refs[t_fold:t_fold + 6]
    ug_ref, zy_ref, hrf_ref, hrb_ref, hif_ref, hib_ref = refs[t_fold + 6:]
    lanes = x_refs[0].shape[1]
    n_grp = lanes // grp
    slots = lanes // grp
    n_half = t_fold // slots
    half = p2 // 2
    lane = lax.broadcasted_iota(jnp.int32, (1, lanes), 1)
    n_chunks = n_rows // row_chunk
    sub = 16

    def gather(rc, _):
        rows = pl.ds(pl.multiple_of(rc * row_chunk, sub), row_chunk)
        halves = [_block_transpose([x_refs[hf * slots + s8][rows, :] for s8 in range(slots)], lane, grp)
                  for hf in range(n_half)]
        for g in range(n_grp):
            ug_ref[g, rows, :] = jnp.concatenate([hv[g] for hv in halves], axis=1).astype(ug_ref.dtype)
        return 0

    lax.fori_loop(0, n_chunks, gather, 0)

    is_f = lax.broadcasted_iota(jnp.int32, (1, p2), 1) < half
    sl = SUBLANES
    tile_row = lax.broadcasted_iota(jnp.int32, (sl, 1), 0)
    first_row = tile_row == jnp.where(is_f, 0, sl - 1)
    n_tiles = n_rows // sub
    n_ctx_tiles = n_ctx_rows // sub

    for blk in range(n_grp // scan_groups):
        gs = [blk * scan_groups + k for k in range(scan_groups)]
        for g in gs:
            zy_ref[g] = jnp.dot(ug_ref[g], win_ref[g], preferred_element_type=F32)

        def step(i, carry, gs=gs):
            tf = pl.multiple_of(i * sub, sub)
            tb = pl.multiple_of(
                jnp.where(i < n_ctx_tiles, n_ctx_tiles - 1 - i, n_tiles - 1 - (i - n_ctx_tiles)) * sub, sub)
            out = []
            for k, g in enumerate(gs):
                c_r, c_i = carry[2 * k], carry[2 * k + 1]
                zrf, zrb = zy_ref[g, pl.ds(tf, sub), 0:p2], zy_ref[g, pl.ds(tb, sub), 0:p2]
                zif, zib = zy_ref[g, pl.ds(tf, sub), p2:2 * p2], zy_ref[g, pl.ds(tb, sub), p2:2 * p2]
                prev_r, prev_i = [], []
                for q in range(sub // sl):
                    fs = slice(q * sl, (q + 1) * sl)
                    bs = slice(sub - (q + 1) * sl, sub - q * sl)
                    v_r = jnp.where(is_f, zrf[fs], zrb[bs])
                    v_i = jnp.where(is_f, zif[fs], zib[bs])
                    for j, dist in enumerate((1, 2, 4)):
                        m_r, m_i = mul_ref[g, 2 * j], mul_ref[g, 2 * j + 1]
                        s_r = jnp.where(is_f, pltpu.roll(v_r, dist, axis=0), pltpu.roll(v_r, sl - dist, axis=0))
                        s_i = jnp.where(is_f, pltpu.roll(v_i, dist, axis=0), pltpu.roll(v_i, sl - dist, axis=0))
                        v_r, v_i = v_r + (m_r * s_r - m_i * s_i), v_i + (m_r * s_i + m_i * s_r)
                    p_r, p_i = mul_ref[g, 6], mul_ref[g, 7]
                    h_r = v_r + (p_r * c_r - p_i * c_i)
                    h_i = v_i + (p_r * c_i + p_i * c_r)
                    e_r = jnp.where(is_f, pltpu.roll(h_r, 1, axis=0), pltpu.roll(h_r, sl - 1, axis=0))
                    e_i = jnp.where(is_f, pltpu.roll(h_i, 1, axis=0), pltpu.roll(h_i, sl - 1, axis=0))
                    prev_r.append(jnp.where(first_row, c_r, e_r))
                    prev_i.append(jnp.where(first_row, c_i, e_i))
                    c_r = jnp.where(is_f, h_r[sl - 1:sl], h_r[0:1])
                    c_i = jnp.where(is_f, h_i[sl - 1:sl], h_i[0:1])
                hrf_ref[k, pl.ds(tf, sub), :] = jnp.concatenate(prev_r, axis=0).astype(hrf_ref.dtype)
                hrb_ref[k, pl.ds(tb, sub), :] = jnp.concatenate(prev_r[::-1], axis=0).astype(hrb_ref.dtype)
                hif_ref[k, pl.ds(tf, sub), :] = jnp.concatenate(prev_i, axis=0).astype(hif_ref.dtype)
                hib_ref[k, pl.ds(tb, sub), :] = jnp.concatenate(prev_i[::-1], axis=0).astype(hib_ref.dtype)
                out += [c_r, c_i]
            return tuple(out)

        zero = jnp.zeros((1, p2), F32)
        lax.fori_loop(0, n_tiles, step, tuple(zero for _ in range(2 * scan_groups)))
        for k, g in enumerate(gs):
            h_r = jnp.where(is_f, hrf_ref[k], hrb_ref[k])
            h_i = jnp.where(is_f, hif_ref[k], hib_ref[k])
            hp = jnp.concatenate([h_r, h_i], axis=1)
            zy_ref[g] = (jnp.dot(ug_ref[g], wintra_ref[g], preferred_element_type=F32)
                         + jnp.dot(hp, wout_ref[g], preferred_element_type=F32))

    def scatter(rc, _):
        rows = pl.ds(pl.multiple_of(rc * row_chunk, sub), row_chunk)
        for hf in range(n_half):
            ys = _block_transpose([zy_ref[g, rows, hf * lanes:(hf + 1) * lanes] for g in range(n_grp)], lane, grp)
            for t8 in range(slots):
                t = hf * slots + t8
                o_ref[t, rows, :] = _gelu_tanh(d_ref[...] * x_refs[t][rows, :] + ys[t8]).astype(o_ref.dtype)
        return 0

    lax.fori_loop(0, n_chunks, scatter, 0)


def s5_scan_act(u, d_skip, weights, ctx_len):
    m, d = u.shape
    w_in, w_intra, w_out, scan_mul = weights
    t_fold, grp, lanes = S5_T, S5_GROUP, 128
    n_rows = m // t_fold
    width = t_fold * grp
    p2 = scan_mul.shape[-1]
    n_grp = lanes // grp
    n_tile = d // lanes
    row_chunk = _pick(n_rows, (80, 16))
    assert n_rows % 16 == 0 and (ctx_len // t_fold) % 16 == 0 and t_fold % n_grp == 0
    kern = functools.partial(_s5_kernel, n_rows=n_rows, n_ctx_rows=ctx_len // t_fold, p2=p2, t_fold=t_fold,
                             grp=grp, scan_groups=S5_GROUP_BLOCK, row_chunk=row_chunk)
    blk = lambda a, b: pl.BlockSpec((n_grp, a, b), lambda i: (i, 0, 0))
    x_specs = [pl.BlockSpec((n_rows, lanes), functools.partial(lambda i, s: (0, s * n_tile + i), s=s),
                            pipeline_mode=pl.Buffered(1)) for s in range(t_fold)]
    u2 = u.reshape(n_rows, t_fold * d)
    out = pl.pallas_call(
        kern,
        grid=(n_tile,),
        in_specs=x_specs + [pl.BlockSpec((1, lanes), lambda i: (0, i)),
                            blk(width, 2 * p2), blk(width, width), blk(2 * p2, width),
                            pl.BlockSpec((n_grp,) + scan_mul.shape[1:], lambda i: (i, 0, 0, 0))],
        out_specs=pl.BlockSpec((t_fold, n_rows, lanes), lambda i: (0, 0, i)),
        out_shape=jax.ShapeDtypeStruct((t_fold, n_rows, d), MXU_DTYPE),
        scratch_shapes=[pltpu.VMEM((n_grp, n_rows, width), MXU_DTYPE), pltpu.VMEM((n_grp, n_rows, 2 * p2), F32)]
        + [pltpu.VMEM((S5_GROUP_BLOCK, n_rows, p2), MXU_DTYPE)] * 4,
        compiler_params=_cparams("parallel"),
        name="s5_scan_act",
    )(*([u2] * t_fold), d_skip.reshape(1, d), w_in, w_intra, w_out, scan_mul)
    return jnp.transpose(out, (1, 0, 2)).reshape(m, d)


def _chunk_index(d, c, n_ctx_chunks, n_chunks):
    back = jnp.where(c < n_ctx_chunks, n_ctx_chunks - 1 - c, n_chunks - 1 - (c - n_ctx_chunks))
    return jnp.where(d == 0, c, back)


def _tri_masks():
    lower = np.tril(np.ones((CHUNK, CHUNK), np.float32))
    return jnp.asarray(np.stack([np.stack([lower, lower.T]), np.stack([lower.T, lower])]))


def _split_hi_lo(x):
    hi = x.astype(jnp.bfloat16)
    lo = (x - hi.astype(F32)).astype(jnp.bfloat16)
    return hi, lo


def _gla_kernel(q_ref, k_ref, v_ref, r_ref, wg_ref, bg_ref, tri_ref, o_ref, s_ref, *, dk, dv, hb):
    @pl.when(pl.program_id(2) == 0)
    def _():
        s_ref[...] = jnp.zeros_like(s_ref)

    w = hb * dk
    x = jnp.dot(r_ref[...].astype(MXU_DTYPE), wg_ref[0].astype(MXU_DTYPE), preferred_element_type=F32) + bg_ref[0]
    la = _log_sigmoid(x) * (1.0 / GLA_TAU)
    tri = tri_ref[0, 0]
    vis = tri > 0
    hi, lo = _split_hi_lo(la)
    bb = jnp.dot(tri.astype(jnp.bfloat16), jnp.concatenate([hi, lo], axis=1), preferred_element_type=F32)
    b = bb[:, :w] + bb[:, w:]
    tot = jnp.sum(la, axis=0, keepdims=True)
    mid = 0.5 * tot
    q = q_ref[...].astype(F32) * dk ** -0.5
    k = k_ref[...].astype(F32)
    qe = (q * jnp.exp(b - mid)).astype(MXU_DTYPE)
    ke = (k * jnp.exp(mid - b)).astype(MXU_DTYPE)
    qb = (q * jnp.exp(b)).astype(MXU_DTYPE)
    kd = (k * jnp.exp(tot - b)).astype(MXU_DTYPE)
    dec = jnp.exp(tot)
    for h in range(hb):
        ks = slice(h * dk, (h + 1) * dk)
        vs = slice(h * dv, (h + 1) * dv)
        v = v_ref[:, vs].astype(MXU_DTYPE)
        att = lax.dot_general(qe[:, ks], ke[:, ks], (((1,), (1,)), ((), ())), preferred_element_type=F32)
        att = jnp.where(vis, att, 0.0).astype(MXU_DTYPE)
        st = s_ref[h]
        o_ref[0, :, vs] = (jnp.dot(att, v, preferred_element_type=F32)
                           + lax.dot_general(qb[:, ks], st.astype(MXU_DTYPE), (((1,), (1,)), ((), ())),
                                             preferred_element_type=F32))
        upd = lax.dot_general(v, kd[:, ks], (((0,), (0,)), ((), ())), preferred_element_type=F32)
        s_ref[h] = st * dec[:, ks] + upd


def gla_scan(proj, r, w_gate_up, b_gate, ctx_len):
    m = proj.shape[0]
    n_dir, rank, qk = w_gate_up.shape
    dk = qk // HEADS
    dv = 2 * dk
    hb = HEAD_BLOCK
    nhb = HEADS // hb
    n_chunks = m // CHUNK
    n_ctx = ctx_len // CHUNK
    wg = jnp.zeros((n_dir, GATE_PAD, qk), F32)
    for dr in range(n_dir):
        wg = wg.at[dr, dr * rank:(dr + 1) * rank].set(w_gate_up[dr])
    cidx = functools.partial(_chunk_index, n_ctx_chunks=n_ctx, n_chunks=n_chunks)
    return pl.pallas_call(
        functools.partial(_gla_kernel, dk=dk, dv=dv, hb=hb),
        grid=(n_dir, nhb, n_chunks),
        in_specs=[
            pl.BlockSpec((CHUNK, hb * dk), lambda d, h, c: (cidx(d, c), h)),
            pl.BlockSpec((CHUNK, hb * dk), lambda d, h, c: (cidx(d, c), nhb + h)),
            pl.BlockSpec((CHUNK, hb * dv), lambda d, h, c: (cidx(d, c), nhb + h)),
            pl.BlockSpec((CHUNK, GATE_PAD), lambda d, h, c: (cidx(d, c), 0)),
            pl.BlockSpec((1, GATE_PAD, hb * dk), lambda d, h, c: (d, 0, h)),
            pl.BlockSpec((1, 1, hb * dk), lambda d, h, c: (d, 0, h)),
            pl.BlockSpec((1, 1, CHUNK, CHUNK), lambda d, h, c: (d, 0, 0, 0)),
        ],
        out_specs=pl.BlockSpec((1, CHUNK, hb * dv), lambda d, h, c: (d, cidx(d, c), h)),
        out_shape=jax.ShapeDtypeStruct((n_dir, m, dv * HEADS), F32),
        scratch_shapes=[pltpu.VMEM((hb, dv, dk), F32)],
        compiler_params=_cparams("arbitrary", "arbitrary", "arbitrary"),
        name="gla_scan",
    )(proj, proj, proj, r, wg, b_gate.reshape(n_dir, 1, qk), _tri_masks())


def _mlstm_kernel(q_ref, k_ref, v_ref, gcol_ref, grow_ref, tri_ref, o_ref, c_ref, m_ref, *, dk, dv, hb):
    d = pl.program_id(0)
    h0 = pl.program_id(1) * hb

    @pl.when(pl.program_id(2) == 0)
    def _():
        c_ref[...] = jnp.zeros_like(c_ref)
        m_ref[...] = jnp.zeros_like(m_ref)

    t = q_ref.shape[0]
    tri = tri_ref[0, 0]
    tri_t = tri_ref[0, 1]
    vis = tri > 0
    lane = lax.broadcasted_iota(jnp.int32, (1, gcol_ref.shape[1]), 1)
    gcol = gcol_ref[...]
    row = lax.broadcasted_iota(jnp.int32, (t, 1), 0)
    last = jnp.where(d == 0, t - 1, 0)
    e0 = (lax.broadcasted_iota(jnp.int32, (t, 128), 1) == 0).astype(MXU_DTYPE)
    for h in range(hb):
        ks = slice(h * dk, (h + 1) * dk)
        vs = slice(h * dv, (h + 1) * dv)
        col_i = 2 * d * HEADS + h0 + h
        li_c = jnp.sum(jnp.where(lane == col_i, gcol, 0.0), axis=1, keepdims=True)
        lf_c = _log_sigmoid(jnp.sum(jnp.where(lane == col_i + HEADS, gcol, 0.0), axis=1, keepdims=True))
        li_r = grow_ref[0, pl.ds(col_i, 1), :]
        lf_r = _log_sigmoid(grow_ref[0, pl.ds(col_i + HEADS, 1), :])
        b_c = jnp.sum(tri * lf_r, axis=1, keepdims=True)
        b_r = jnp.sum(tri_t * lf_c, axis=0, keepdims=True)
        b_last = jnp.sum(lf_r, axis=1, keepdims=True)
        m0 = m_ref[h][:, 0:1]
        logw = jnp.where(vis, b_c - b_r + li_r, -jnp.inf)
        w_inter = b_c + m0
        m_t = jnp.maximum(w_inter, jnp.max(logw, axis=1, keepdims=True))
        m_new = jnp.max(jnp.where(row == last, m_t, -jnp.inf), axis=0, keepdims=True)
        carry = jnp.exp(b_last + m0 - m_new)
        w_last_c = jnp.exp(b_last - b_c + li_c - m_new)
        q = q_ref[:, ks].astype(MXU_DTYPE)
        kf = k_ref[:, ks].astype(F32) * dk ** -0.5
        v_ext = jnp.concatenate([v_ref[:, vs].astype(MXU_DTYPE), e0], axis=1)
        c0 = c_ref[h]
        qk = lax.dot_general(q, kf.astype(MXU_DTYPE), (((1,), (1,)), ((), ())), preferred_element_type=F32)
        qk = qk * jnp.exp(logw - m_t)
        inter = jnp.exp(w_inter - m_t)
        num = (jnp.dot(qk.astype(MXU_DTYPE), v_ext, preferred_element_type=F32)
               + inter * jnp.dot(q, c0.astype(MXU_DTYPE), preferred_element_type=F32))
        den = num[:, dv:dv + 1]
        o_ref[0, :, vs] = num[:, :dv] / jnp.maximum(jnp.abs(den), jnp.exp(-m_t))
        kw = (kf * w_last_c).astype(MXU_DTYPE)
        c_ref[h] = carry * c0 + lax.dot_general(kw, v_ext, (((0,), (0,)), ((), ())), preferred_element_type=F32)
        m_ref[h] = jnp.broadcast_to(m_new, m_ref.shape[1:])


def mlstm_scan(proj, gates, ctx_len):
    m = proj.shape[0]
    dk = proj.shape[1] // (6 * HEADS)
    dv = 2 * dk
    hb = HEAD_BLOCK
    nhb = HEADS // hb
    n_chunks = m // CHUNK
    n_ctx = ctx_len // CHUNK
    n_gate = 4 * HEADS
    grow = jnp.transpose(gates[:, :n_gate].reshape(n_chunks, CHUNK, n_gate), (0, 2, 1))
    cidx = functools.partial(_chunk_index, n_ctx_chunks=n_ctx, n_chunks=n_chunks)
    return pl.pallas_call(
        functools.partial(_mlstm_kernel, dk=dk, dv=dv, hb=hb),
        grid=(2, nhb, n_chunks),
        in_specs=[
            pl.BlockSpec((CHUNK, hb * dk), lambda d, h, c: (cidx(d, c), h)),
            pl.BlockSpec((CHUNK, hb * dk), lambda d, h, c: (cidx(d, c), nhb + h)),
            pl.BlockSpec((CHUNK, hb * dv), lambda d, h, c: (cidx(d, c), nhb + h)),
            pl.BlockSpec((CHUNK, GATE_PAD), lambda d, h, c: (cidx(d, c), 0)),
            pl.BlockSpec((1, n_gate, CHUNK), lambda d, h, c: (cidx(d, c), 0, 0)),
            pl.BlockSpec((1, 2, CHUNK, CHUNK), lambda d, h, c: (d, 0, 0, 0)),
        ],
        out_specs=pl.BlockSpec((1, CHUNK, hb * dv), lambda d, h, c: (d, cidx(d, c), h)),
        out_shape=jax.ShapeDtypeStruct((2, m, dv * HEADS), F32),
        scratch_shapes=[pltpu.VMEM((hb, dk, dv + 128), F32), pltpu.VMEM((hb, 1, 128), F32)],
        compiler_params=_cparams("arbitrary", "arbitrary", "arbitrary"),
        name="mlstm_scan",
    )(proj, proj, proj, gates, grow, _tri_masks())


def _pad_cols(w, width):
    return jnp.pad(w, ((0, 0), (0, width - w.shape[1])))


def _s5_layer(u, p, j, ctx_len):
    weights = _s5_weights(p['s5_lam_re'][j], p['s5_lam_im'][j], p['s5_log_step'][j], p['s5_b_re'][j],
                          p['s5_b_im'][j], p['s5_c_re'][j], p['s5_c_im'][j])
    act = s5_scan_act(u, p['s5_d'][j], weights, ctx_len)
    return matmul(act, p['s5_w_glu'][j].astype(MXU_DTYPE), p['s5_b_glu'][j], F32)


def _gla_layer(u, p, j, ctx_len):
    w_in = p['gla_w_in'][j]
    n_main = w_in.shape[1] - 2 * p['gla_w_gate_up'].shape[2]
    proj = matmul(u, w_in[:, :n_main].astype(MXU_DTYPE), jnp.zeros((n_main,), F32), MXU_DTYPE)
    r = matmul(u, _pad_cols(w_in[:, n_main:], GATE_PAD).astype(MXU_DTYPE), jnp.zeros((GATE_PAD,), F32), F32)
    o2 = gla_scan(proj, r, p['gla_w_gate_up'][j], p['gla_b_gate'][j], ctx_len)
    on = head_gate(o2, proj, 2, p['gla_norm_g'][j], sigmoid_gate=False)
    d = u.shape[1]
    return matmul(on, p['gla_w_out'][j].astype(MXU_DTYPE), jnp.zeros((d,), F32), F32)


def _mlstm_layer(u, p, j, ctx_len):
    w_in = p['mlstm_w_in'][j]
    n_gate = 4 * HEADS
    n_main = w_in.shape[1] - n_gate
    proj = matmul(u, w_in[:, :n_main].astype(MXU_DTYPE), jnp.zeros((n_main,), F32), MXU_DTYPE)
    b_gate = jnp.pad(p['mlstm_b_gate'][j].reshape(n_gate), (0, GATE_PAD - n_gate))
    gates = matmul(u, _pad_cols(w_in[:, n_main:], GATE_PAD).astype(MXU_DTYPE), b_gate, F32)
    h2 = mlstm_scan(proj, gates, ctx_len)
    hn = head_gate(h2, proj, 2, p['mlstm_norm_g'][j], sigmoid_gate=True)
    d = u.shape[1]
    return matmul(hn, p['mlstm_w_out'][j].astype(MXU_DTYPE), jnp.zeros((d,), F32), F32)


def kernel(x, c, ctx, c_ctx, ada_w_down, ada_w_up, ada_b, norm_g, ffn_w_in, ffn_conv_w, ffn_conv_b, ffn_w_out, s5_lam_re, s5_lam_im, s5_log_step, s5_b_re, s5_b_im, s5_c_re, s5_c_im, s5_d, s5_w_glu, s5_b_glu, gla_w_in, gla_w_gate_up, gla_b_gate, gla_norm_g, gla_w_out, mlstm_w_in, mlstm_b_gate, mlstm_norm_g, mlstm_w_out):
    p = dict(s5_lam_re=s5_lam_re, s5_lam_im=s5_lam_im, s5_log_step=s5_log_step, s5_b_re=s5_b_re, s5_b_im=s5_b_im,
             s5_c_re=s5_c_re, s5_c_im=s5_c_im, s5_d=s5_d, s5_w_glu=s5_w_glu, s5_b_glu=s5_b_glu,
             gla_w_in=gla_w_in, gla_w_gate_up=gla_w_gate_up, gla_b_gate=gla_b_gate, gla_norm_g=gla_norm_g,
             gla_w_out=gla_w_out, mlstm_w_in=mlstm_w_in, mlstm_b_gate=mlstm_b_gate, mlstm_norm_g=mlstm_norm_g,
             mlstm_w_out=mlstm_w_out)
    bsz, seq, d = x.shape
    ctx_len = ctx.shape[1]
    depth = ada_w_down.shape[0]
    assert bsz == 1 and ctx_len == ROW_TILE and seq % (GRID_W * (ROW_TILE // GRID_W)) == 0
    cond = jnp.zeros((8, d), F32).at[0].set(c_ctx).at[1].set(c[0])
    mods = ada_all(cond, ada_w_down, ada_w_up, ada_b)[:, :2].reshape(depth, 2, 6, 1, d)
    mod = lambda i, n: mods[i, :, n]
    mixer_dtype = lambda i: F32 if i % 3 == 0 else MXU_DTYPE
    h = jnp.concatenate([ctx[0], x[0]], axis=0)
    u = norm_mod(h, norm_g[0, 0], mod(0, 1), mod(0, 0), ctx_len, mixer_dtype(0))
    f2 = ffn_w_in.shape[2]
    for i in range(depth):
        kind, j = i % 3, i // 3
        ffn_in = (norm_g[i, 2], mod(i, 4), mod(i, 3), MXU_DTYPE)
        if kind == 0:
            h, uf = resid(h, _s5_layer(u, p, j, ctx_len), norm_g[i, 1], mod(i, 2), ctx_len, glu=True, nxt=ffn_in)
        else:
            y = _gla_layer(u, p, j, ctx_len) if kind == 1 else _mlstm_layer(u, p, j, ctx_len)
            h, uf = resid(h, y, norm_g[i, 1], mod(i, 2), ctx_len, nxt=ffn_in)
        gu = matmul(uf, ffn_w_in[i].astype(MXU_DTYPE), jnp.zeros((f2,), F32), MXU_DTYPE)
        act = conv_act(gu, ffn_conv_w[i], ffn_conv_b[i], ctx_len)
        f = matmul(act, ffn_w_out[i].astype(MXU_DTYPE), jnp.zeros((d,), F32), F32)
        if i + 1 < depth:
            nxt = (norm_g[i + 1, 0], mod(i + 1, 1), mod(i + 1, 0), mixer_dtype(i + 1))
            h, u = resid(h, f, norm_g[i, 3], mod(i, 5), ctx_len, nxt=nxt)
        else:
            h = resid(h, f, norm_g[i, 3], mod(i, 5), ctx_len, drop_ctx=True)
    return h.reshape(bsz, seq, d)
```

```python
import functools
import math

import numpy as np
import jax
import jax.numpy as jnp
from jax import lax
from jax.experimental import pallas as pl
from jax.experimental.pallas import tpu as pltpu

EPS = 1e-6
GRID_W = 64
S5_GROUP = 16
S5_T = 16
S5_GROUP_BLOCK = 4
HEADS = 8
HEAD_BLOCK = 4
CHUNK = 128
ML_CHUNK = 128
GLA_TAU = 16.0
GATE_PAD = 128
ROW_TILE = 256
SUBLANES = 8
MXU_DTYPE = jnp.bfloat16
V7X_VMEM_LIMIT_BYTES = 48 * 1024 * 1024

F32 = jnp.float32


def _cparams(*sem):
    return pltpu.CompilerParams(dimension_semantics=sem, vmem_limit_bytes=V7X_VMEM_LIMIT_BYTES)


def _pick(dim, candidates):
    for c in candidates:
        if dim % c == 0:
            return c
    raise ValueError(f"no tile for {dim} in {candidates}")


def _log_sigmoid(x):
    return jnp.minimum(x, 0.0) - jnp.log1p(jnp.exp(-jnp.abs(x)))


def _sigmoid(x):
    return 1.0 / (1.0 + jnp.exp(-x))


def _silu(x):
    return x * _sigmoid(x)


def _gelu_tanh(x):
    return 0.5 * x * (1.0 + jnp.tanh(math.sqrt(2.0 / math.pi) * (x + 0.044715 * (x * x * x))))


def _ada_kernel(cond_ref, wd_ref, wu_ref, b_ref, o_ref):
    s = _silu(cond_ref[...])
    t = jnp.dot(s, wd_ref[0], preferred_element_type=F32, precision=lax.Precision.HIGHEST)
    o_ref[0] = jnp.dot(t, wu_ref[0], preferred_element_type=F32, precision=lax.Precision.HIGHEST) + b_ref[0]


def ada_all(cond, w_down, w_up, b):
    depth, d, rank = w_down.shape
    n = w_up.shape[2]
    tn = _pick(n, (2048, 1024, 512, 256, 128))
    return pl.pallas_call(
        _ada_kernel,
        grid=(depth, n // tn),
        in_specs=[
            pl.BlockSpec((8, d), lambda l, j: (0, 0)),
            pl.BlockSpec((1, d, rank), lambda l, j: (l, 0, 0)),
            pl.BlockSpec((1, rank, tn), lambda l, j: (l, 0, j)),
            pl.BlockSpec((1, 1, tn), lambda l, j: (l, 0, j)),
        ],
        out_specs=pl.BlockSpec((1, 8, tn), lambda l, j: (l, 0, j)),
        out_shape=jax.ShapeDtypeStruct((depth, 8, n), F32),
        compiler_params=_cparams("arbitrary", "arbitrary"),
        name="ada_all",
    )(cond, w_down, w_up, b.reshape(depth, 1, n))


def _rms(x, g):
    return x * lax.rsqrt(jnp.mean(x * x, axis=-1, keepdims=True) + EPS) * g


def _norm_mod_kernel(x_ref, g_ref, sc_ref, sh_ref, o_ref):
    y = _rms(x_ref[...], g_ref[...])
    o_ref[...] = (y * (1.0 + sc_ref[0]) + sh_ref[0]).astype(o_ref.dtype)


def _row_specs(d, ctx_tiles, row_offset=0):
    row = pl.BlockSpec((ROW_TILE, d), lambda i: (i + row_offset, 0))
    vec = pl.BlockSpec((1, d), lambda i: (0, 0))
    mod = pl.BlockSpec((1, 1, d), lambda i: (jnp.where(i + row_offset < ctx_tiles, 0, 1), 0, 0))
    return row, vec, mod


def norm_mod(x, g, scale, shift, ctx_len, out_dtype):
    m, d = x.shape
    row, vec, mod = _row_specs(d, ctx_len // ROW_TILE)
    return pl.pallas_call(
        _norm_mod_kernel,
        grid=(m // ROW_TILE,),
        in_specs=[row, vec, mod, mod],
        out_specs=row,
        out_shape=jax.ShapeDtypeStruct((m, d), out_dtype),
        compiler_params=_cparams("parallel"),
        name="norm_mod",
    )(x, g.reshape(1, d), scale, shift)


def _resid_kernel(*refs, glu, has_next):
    h_ref, y_ref = refs[0], refs[1]
    n_in = 2 + (1 if glu else 0)
    g_ref, gate_ref = refs[n_in], refs[n_in + 1]
    y = y_ref[...].astype(F32)
    if glu:
        y = y * _sigmoid(refs[2][...].astype(F32))
    h = h_ref[...] + gate_ref[0] * _rms(y, g_ref[...])
    if has_next:
        gn_ref, sc_ref, sh_ref, o_ref, u_ref = refs[n_in + 2:]
        u_ref[...] = (_rms(h, gn_ref[...]) * (1.0 + sc_ref[0]) + sh_ref[0]).astype(u_ref.dtype)
    else:
        o_ref = refs[n_in + 2]
    o_ref[...] = h


def resid(h, y, g, gate, ctx_len, *, glu=False, nxt=None, drop_ctx=False):
    m, d = h.shape
    ctx_tiles = ctx_len // ROW_TILE
    off = ctx_tiles if drop_ctx else 0
    row, vec, mod = _row_specs(d, ctx_tiles, off)
    out_row = pl.BlockSpec((ROW_TILE, d), lambda i: (i, 0))
    m_out = m - off * ROW_TILE
    ins, in_specs = [h, y], [row, row]
    if glu:
        ins.append(y)
        in_specs.append(pl.BlockSpec((ROW_TILE, d), lambda i: (i + off, 1)))
    ins += [g.reshape(1, d), gate]
    in_specs += [vec, mod]
    out_shape = [jax.ShapeDtypeStruct((m_out, d), F32)]
    out_specs = [out_row]
    if nxt is not None:
        g_next, scale, shift, dtype = nxt
        ins += [g_next.reshape(1, d), scale, shift]
        in_specs += [vec, mod, mod]
        out_shape.append(jax.ShapeDtypeStruct((m_out, d), dtype))
        out_specs.append(out_row)
    res = pl.pallas_call(
        functools.partial(_resid_kernel, glu=glu, has_next=nxt is not None),
        grid=(m_out // ROW_TILE,),
        in_specs=in_specs,
        out_specs=out_specs,
        out_shape=out_shape,
        compiler_params=_cparams("parallel"),
        name="resid",
    )(*ins)
    return res if nxt is not None else res[0]


def _head_gate_kernel(o_ref, gate_ref, g_ref, out_ref, *, heads, sigmoid_gate):
    o = o_ref[0] + o_ref[1]
    dv = o.shape[1] // heads
    for h in range(heads):
        sl = slice(h * dv, (h + 1) * dv)
        gt = gate_ref[:, sl].astype(F32)
        act = _sigmoid(gt) if sigmoid_gate else _silu(gt)
        out_ref[:, sl] = (_rms(o[:, sl], g_ref[...]) * act).astype(out_ref.dtype)


def head_gate(o2, proj, gate_col_block, norm_g, sigmoid_gate):
    _, m, d = o2.shape
    dv = d // HEADS
    return pl.pallas_call(
        functools.partial(_head_gate_kernel, heads=HEADS, sigmoid_gate=sigmoid_gate),
        grid=(m // ROW_TILE,),
        in_specs=[
            pl.BlockSpec((2, ROW_TILE, d), lambda i: (0, i, 0)),
            pl.BlockSpec((ROW_TILE, d), lambda i: (i, gate_col_block)),
            pl.BlockSpec((1, dv), lambda i: (0, 0)),
        ],
        out_specs=pl.BlockSpec((ROW_TILE, d), lambda i: (i, 0)),
        out_shape=jax.ShapeDtypeStruct((m, d), MXU_DTYPE),
        compiler_params=_cparams("parallel"),
        name="head_gate",
    )(o2, proj, norm_g.reshape(1, dv))


def _mm_kernel(a_ref, w_ref, b_ref, o_ref, acc_ref, *, nk):
    k = pl.program_id(2)

    @pl.when(k == 0)
    def _():
        acc_ref[...] = jnp.zeros_like(acc_ref)

    acc_ref[...] += jnp.dot(a_ref[...].astype(MXU_DTYPE), w_ref[...], preferred_element_type=F32)

    @pl.when(k == nk - 1)
    def _():
        o_ref[...] = (acc_ref[...] + b_ref[...]).astype(o_ref.dtype)


def _mm_full_k_kernel(a_ref, w_ref, b_ref, o_ref):
    acc = jnp.dot(a_ref[...].astype(MXU_DTYPE), w_ref[...], preferred_element_type=F32)
    o_ref[...] = (acc + b_ref[...]).astype(o_ref.dtype)


MM_FULL_K_MAX = 4096


def matmul(a, w, bias, out_dtype):
    m, k = a.shape
    n = w.shape[1]
    tm = _pick(m, (1280, 1024, 640, 512, 256))
    tn = _pick(n, (1024, 512, 256, 128))
    bias2 = bias.reshape(1, n).astype(F32)
    if k <= MM_FULL_K_MAX:
        return pl.pallas_call(
            _mm_full_k_kernel,
            grid=(m // tm, n // tn),
            in_specs=[
                pl.BlockSpec((tm, k), lambda i, j: (i, 0), pipeline_mode=pl.Buffered(1)),
                pl.BlockSpec((k, tn), lambda i, j: (0, j)),
                pl.BlockSpec((1, tn), lambda i, j: (0, j)),
            ],
            out_specs=pl.BlockSpec((tm, tn), lambda i, j: (i, j)),
            out_shape=jax.ShapeDtypeStruct((m, n), out_dtype),
            compiler_params=_cparams("parallel", "arbitrary"),
            name="matmul_full_k",
        )(a, w, bias2)
    tk = _pick(k, (2048, 1024, 512, 256, 128))
    nk = k // tk
    return pl.pallas_call(
        functools.partial(_mm_kernel, nk=nk),
        grid=(m // tm, n // tn, nk),
        in_specs=[
            pl.BlockSpec((tm, tk), lambda i, j, kk: (i, kk)),
            pl.BlockSpec((tk, tn), lambda i, j, kk: (kk, j)),
            pl.BlockSpec((1, tn), lambda i, j, kk: (0, j)),
        ],
        out_specs=pl.BlockSpec((tm, tn), lambda i, j, kk: (i, j)),
        out_shape=jax.ShapeDtypeStruct((m, n), out_dtype),
        scratch_shapes=[pltpu.VMEM((tm, tn), F32)],
        compiler_params=_cparams("parallel", "parallel", "arbitrary"),
        name="matmul",
    )(a, w, bias2)


def _cast_kernel(x_ref, o_ref):
    o_ref[...] = x_ref[...].astype(o_ref.dtype)


def cast_cols(w, n_cols, dtype):
    k = w.shape[0]
    tk = _pick(k, (512, 256, 128))
    tn = _pick(n_cols, (2048, 1024, 512, 256, 128))
    return pl.pallas_call(
        _cast_kernel,
        grid=(k // tk, n_cols // tn),
        in_specs=[pl.BlockSpec((tk, tn), lambda i, j: (i, j))],
        out_specs=pl.BlockSpec((tk, tn), lambda i, j: (i, j)),
        out_shape=jax.ShapeDtypeStruct((k, n_cols), dtype),
        compiler_params=_cparams("parallel", "parallel"),
        name="cast_cols",
    )(w)


def _conv_act_kernel(gp_ref, gc_ref, gn_ref, up_ref, w_ref, b_ref, sh_ref, o_ref, *, tb, ctx_tiles, grid_w):
    i = pl.program_id(0)
    n_tiles = pl.num_programs(0)
    up = up_ref[...].astype(F32)

    def finish(conv):
        o_ref[...] = (_silu(conv + b_ref[...]) * up).astype(o_ref.dtype)

    def tap(dr, dc):
        k = 3 * (dr + 1) + (dc + 1)
        return w_ref[k:k + 1, :]

    @pl.when(i < ctx_tiles)
    def _():
        g = gc_ref[...].astype(F32)
        pos = lax.broadcasted_iota(jnp.int32, (tb, 1), 0)
        left = jnp.where(pos > 0, pltpu.roll(g, 1, axis=0), 0.0)
        right = jnp.where(pos < tb - 1, pltpu.roll(g, tb - 1, axis=0), 0.0)
        finish(tap(0, -1) * left + tap(0, 0) * g + tap(0, 1) * right)

    @pl.when(i >= ctx_tiles)
    def _():
        zero = jnp.zeros_like(gp_ref[...])
        blocks = ([jnp.where(i > ctx_tiles, gp_ref[...], zero)]
                  + [gc_ref[r * grid_w:(r + 1) * grid_w, :] for r in range(tb // grid_w)]
                  + [jnp.where(i < n_tiles - 1, gn_ref[...], zero)])
        shifts = sh_ref[...]
        src = []
        for blk in blocks:
            lr = jnp.dot(shifts, blk, preferred_element_type=F32)
            src.append({-1: lr[:grid_w], 0: blk.astype(F32), 1: lr[grid_w:]})
        rows = []
        for r in range(tb // grid_w):
            conv = None
            for dr in (-1, 0, 1):
                for dc in (-1, 0, 1):
                    term = tap(dr, dc) * src[r + 1 + dr][dc]
                    conv = term if conv is None else conv + term
            rows.append(conv)
        finish(jnp.concatenate(rows, axis=0))


def _shift_mats(n, dtype):
    left = np.eye(n, k=-1, dtype=np.float32)
    return jnp.asarray(np.concatenate([left, left.T], axis=0), dtype)


def conv_act(gu, conv_w, conv_b, ctx_len):
    m, f2 = gu.shape
    f = f2 // 2
    tb = ROW_TILE
    assert GRID_W & (GRID_W - 1) == 0 and tb % GRID_W == 0 and ctx_len == tb
    tc = _pick(f, (1024, 512, 256, 128))
    nblk = m // GRID_W
    r = tb // GRID_W
    kern = functools.partial(_conv_act_kernel, tb=tb, ctx_tiles=ctx_len // tb, grid_w=GRID_W)
    return pl.pallas_call(
        kern,
        grid=(m // tb, f // tc),
        in_specs=[
            pl.BlockSpec((GRID_W, tc), lambda i, j: (jnp.maximum(i * r - 1, 0), j)),
            pl.BlockSpec((tb, tc), lambda i, j: (i, j)),
            pl.BlockSpec((GRID_W, tc), lambda i, j: (jnp.minimum((i + 1) * r, nblk - 1), j)),
            pl.BlockSpec((tb, tc), lambda i, j: (i, j + f // tc)),
            pl.BlockSpec((9, tc), lambda i, j: (0, j)),
            pl.BlockSpec((1, tc), lambda i, j: (0, j)),
            pl.BlockSpec((2 * GRID_W, GRID_W), lambda i, j: (0, 0)),
        ],
        out_specs=pl.BlockSpec((tb, tc), lambda i, j: (i, j)),
        out_shape=jax.ShapeDtypeStruct((m, f), MXU_DTYPE),
        compiler_params=_cparams("parallel", "parallel"),
        name="conv_act",
    )(gu, gu, gu, gu, conv_w.reshape(9, f), conv_b.reshape(1, f), _shift_mats(GRID_W, gu.dtype))


def _s5_weights(lam_re, lam_im, log_step, b_re, b_im, c_re, c_im):
    t = S5_T
    lam = lax.complex(lam_re.astype(F32), lam_im.astype(F32))
    step = jnp.exp(log_step.astype(F32))[:, :, None]
    pw = jnp.exp(jnp.arange(t + 1, dtype=F32)[:, None, None, None] * (lam * step)[None])
    lam_bar = pw[1]
    b_bar = ((lam_bar - 1.0) / lam)[..., None] * lax.complex(b_re.astype(F32), b_im.astype(F32))
    c_mat = lax.complex(c_re.astype(F32), c_im.astype(F32))
    g_n, p_n, i_n = b_bar.shape[1], b_bar.shape[2], b_bar.shape[3]
    kern = jnp.real(jnp.einsum('dgip,tdgp,dgpj->tdgij', c_mat, pw[:t], b_bar,
                               precision=lax.Precision.HIGHEST))
    kf = jnp.transpose(kern[:, 0], (1, 3, 0, 2))
    kb = jnp.transpose(kern[::-1, 1], (1, 3, 0, 2))
    zero = jnp.zeros_like(kf[:, :, :1])
    lags = jnp.concatenate([zero, kb, zero.repeat(t - 1, axis=2)], axis=2) \
        + jnp.concatenate([zero.repeat(t, axis=2), kf], axis=2)
    period = jnp.concatenate([lags, zero], axis=2).reshape(g_n, i_n, (2 * t + 1) * i_n)
    skew = jnp.tile(period, (1, 1, t))[:, :, :t * 2 * t * i_n].reshape(g_n, i_n, t, 2 * t, i_n)
    w_intra = jnp.transpose(skew[:, :, :, t:], (0, 2, 1, 3, 4)).reshape(g_n, t * i_n, t * i_n)
    vf = pw[t - 1 - np.arange(t), 0][..., None] * b_bar[0][None]
    vb = pw[np.arange(t), 1][..., None] * b_bar[1][None]
    w_in = jnp.concatenate([jnp.real(vf), jnp.real(vb), jnp.imag(vf), jnp.imag(vb)], axis=2)
    w_in = jnp.transpose(w_in, (1, 0, 3, 2)).reshape(g_n, t * i_n, 4 * p_n)
    qf = c_mat[0][None] * pw[1 + np.arange(t), 0][:, :, None, :]
    qb = c_mat[1][None] * pw[t - np.arange(t), 1][:, :, None, :]
    w_out = jnp.concatenate([jnp.real(qf), jnp.real(qb), -jnp.imag(qf), -jnp.imag(qb)], axis=3)
    w_out = jnp.transpose(w_out, (1, 3, 0, 2)).reshape(g_n, 4 * p_n, t * i_n)
    sl = SUBLANES
    apow = jnp.exp(((1 + jnp.arange(sl, dtype=F32)) * t)[:, None, None, None] * (lam * step)[None])
    rows = np.arange(sl)
    both = lambda f, b: jnp.concatenate([f, b], axis=-1)
    planes = []
    for dist in (1, 2, 4):
        mf = apow[dist - 1, 0][:, None, :] * jnp.asarray(rows >= dist, F32)[None, :, None]
        mb = apow[dist - 1, 1][:, None, :] * jnp.asarray(rows <= sl - 1 - dist, F32)[None, :, None]
        planes.append(both(mf, mb))
    planes.append(both(jnp.transpose(apow[rows, 0], (1, 0, 2)), jnp.transpose(apow[sl - 1 - rows, 1], (1, 0, 2))))
    scan_mul = jnp.stack([f(pl_) for pl_ in planes for f in (jnp.real, jnp.imag)], axis=1)
    return (w_in.astype(MXU_DTYPE), w_intra.astype(MXU_DTYPE), w_out.astype(MXU_DTYPE), scan_mul)


def _block_transpose(arrs, lane, width):
    n = len(arrs)
    lanes = n * width
    dist = n // 2
    while dist >= 1:
        w = dist * width
        low = jnp.bitwise_and(lane, 2 * w - 1) < w
        nxt = list(arrs)
        for a in range(n):
            if (a // dist) % 2 == 0:
                b = a + dist
                nxt[a] = jnp.where(low, arrs[a], pltpu.roll(arrs[b], w, axis=1))
                nxt[b] = jnp.where(low, pltpu.roll(arrs[a], lanes - w, axis=1), arrs[b])
        arrs = nxt
        dist //= 2
    return arrs


def _s5_kernel(*refs, n_rows, n_ctx_rows, p2, t_fold, grp, scan_groups, row_chunk):
    x_refs = refs[:t_fold]
    d_ref, win_ref, wintra_ref, wout_ref, mul_ref, o_ref = refs[t_fold:t_fold + 6]
    ug_ref, zy_ref, hrf_ref, hrb_ref, hif_ref, hib_ref = refs[t_fold + 6:]
    lanes = x_refs[0].shape[1]
    n_grp = lanes // grp
    slots = lanes // grp
    n_half = t_fold // slots
    half = p2 // 2
    lane = lax.broadcasted_iota(jnp.int32, (1, lanes), 1)
    n_chunks = n_rows // row_chunk
    sub = 16

    def gather(rc, _):
        rows = pl.ds(pl.multiple_of(rc * row_chunk, sub), row_chunk)
        halves = [_block_transpose([x_refs[hf * slots + s8][rows, :] for s8 in range(slots)], lane, grp)
                  for hf in range(n_half)]
        for g in range(n_grp):
            ug_ref[g, rows, :] = jnp.concatenate([hv[g] for hv in halves], axis=1).astype(ug_ref.dtype)
        return 0

    lax.fori_loop(0, n_chunks, gather, 0)

    is_f = lax.broadcasted_iota(jnp.int32, (1, p2), 1) < half
    sl = SUBLANES
    tile_row = lax.broadcasted_iota(jnp.int32, (sl, 1), 0)
    first_row = tile_row == jnp.where(is_f, 0, sl - 1)
    n_tiles = n_rows // sub
    n_ctx_tiles = n_ctx_rows // sub

    for blk in range(n_grp // scan_groups):
        gs = [blk * scan_groups + k for k in range(scan_groups)]
        for g in gs:
            zy_ref[g] = jnp.dot(ug_ref[g], win_ref[g], preferred_element_type=F32)

        def step(i, carry, gs=gs):
            tf = pl.multiple_of(i * sub, sub)
            tb = pl.multiple_of(
                jnp.where(i < n_ctx_tiles, n_ctx_tiles - 1 - i, n_tiles - 1 - (i - n_ctx_tiles)) * sub, sub)
            out = []
            for k, g in enumerate(gs):
                c_r, c_i = carry[2 * k], carry[2 * k + 1]
                zrf, zrb = zy_ref[g, pl.ds(tf, sub), 0:p2], zy_ref[g, pl.ds(tb, sub), 0:p2]
                zif, zib = zy_ref[g, pl.ds(tf, sub), p2:2 * p2], zy_ref[g, pl.ds(tb, sub), p2:2 * p2]
                prev_r, prev_i = [], []
                for q in range(sub // sl):
                    fs = slice(q * sl, (q + 1) * sl)
                    bs = slice(sub - (q + 1) * sl, sub - q * sl)
                    v_r = jnp.where(is_f, zrf[fs], zrb[bs])
                    v_i = jnp.where(is_f, zif[fs], zib[bs])
                    for j, dist in enumerate((1, 2, 4)):
                        m_r, m_i = mul_ref[g, 2 * j], mul_ref[g, 2 * j + 1]
                        s_r = jnp.where(is_f, pltpu.roll(v_r, dist, axis=0), pltpu.roll(v_r, sl - dist, axis=0))
                        s_i = jnp.where(is_f, pltpu.roll(v_i, dist, axis=0), pltpu.roll(v_i, sl - dist, axis=0))
                        v_r, v_i = v_r + (m_r * s_r - m_i * s_i), v_i + (m_r * s_i + m_i * s_r)
                    p_r, p_i = mul_ref[g, 6], mul_ref[g, 7]
                    h_r = v_r + (p_r * c_r - p_i * c_i)
                    h_i = v_i + (p_r * c_i + p_i * c_r)
                    e_r = jnp.where(is_f, pltpu.roll(h_r, 1, axis=0), pltpu.roll(h_r, sl - 1, axis=0))
                    e_i = jnp.where(is_f, pltpu.roll(h_i, 1, axis=0), pltpu.roll(h_i, sl - 1, axis=0))
                    prev_r.append(jnp.where(first_row, c_r, e_r))
                    prev_i.append(jnp.where(first_row, c_i, e_i))
                    c_r = jnp.where(is_f, h_r[sl - 1:sl], h_r[0:1])
                    c_i = jnp.where(is_f, h_i[sl - 1:sl], h_i[0:1])
                hrf_ref[k, pl.ds(tf, sub), :] = jnp.concatenate(prev_r, axis=0).astype(hrf_ref.dtype)
                hrb_ref[k, pl.ds(tb, sub), :] = jnp.concatenate(prev_r[::-1], axis=0).astype(hrb_ref.dtype)
                hif_ref[k, pl.ds(tf, sub), :] = jnp.concatenate(prev_i, axis=0).astype(hif_ref.dtype)
                hib_ref[k, pl.ds(tb, sub), :] = jnp.concatenate(prev_i[::-1], axis=0).astype(hib_ref.dtype)
                out += [c_r, c_i]
            return tuple(out)

        zero = jnp.zeros((1, p2), F32)
        lax.fori_loop(0, n_tiles, step, tuple(zero for _ in range(2 * scan_groups)))
        for k, g in enumerate(gs):
            h_r = jnp.where(is_f, hrf_ref[k], hrb_ref[k])
            h_i = jnp.where(is_f, hif_ref[k], hib_ref[k])
            hp = jnp.concatenate([h_r, h_i], axis=1)
            zy_ref[g] = (jnp.dot(ug_ref[g], wintra_ref[g], preferred_element_type=F32)
                         + jnp.dot(hp, wout_ref[g], preferred_element_type=F32))

    def scatter(rc, _):
        rows = pl.ds(pl.multiple_of(rc * row_chunk, sub), row_chunk)
        for hf in range(n_half):
            ys = _block_transpose([zy_ref[g, rows, hf * lanes:(hf + 1) * lanes] for g in range(n_grp)], lane, grp)
            for t8 in range(slots):
                t = hf * slots + t8
                o_ref[t, rows, :] = _gelu_tanh(d_ref[...] * x_refs[t][rows, :] + ys[t8]).astype(o_ref.dtype)
        return 0

    lax.fori_loop(0, n_chunks, scatter, 0)


def s5_scan_act(u, d_skip, weights, ctx_len):
    m, d = u.shape
    w_in, w_intra, w_out, scan_mul = weights
    t_fold, grp, lanes = S5_T, S5_GROUP, 128
    n_rows = m // t_fold
    width = t_fold * grp
    p2 = scan_mul.shape[-1]
    n_grp = lanes // grp
    n_tile = d // lanes
    row_chunk = _pick(n_rows, (80, 16))
    assert n_rows % 16 == 0 and (ctx_len // t_fold) % 16 == 0 and t_fold % n_grp == 0
    kern = functools.partial(_s5_kernel, n_rows=n_rows, n_ctx_rows=ctx_len // t_fold, p2=p2, t_fold=t_fold,
                             grp=grp, scan_groups=S5_GROUP_BLOCK, row_chunk=row_chunk)
    blk = lambda a, b: pl.BlockSpec((n_grp, a, b), lambda i: (i, 0, 0))
    x_specs = [pl.BlockSpec((n_rows, lanes), functools.partial(lambda i, s: (0, s * n_tile + i), s=s),
                            pipeline_mode=pl.Buffered(1)) for s in range(t_fold)]
    u2 = u.reshape(n_rows, t_fold * d)
    out = pl.pallas_call(
        kern,
        grid=(n_tile,),
        in_specs=x_specs + [pl.BlockSpec((1, lanes), lambda i: (0, i)),
                            blk(width, 2 * p2), blk(width, width), blk(2 * p2, width),
                            pl.BlockSpec((n_grp,) + scan_mul.shape[1:], lambda i: (i, 0, 0, 0))],
        out_specs=pl.BlockSpec((t_fold, n_rows, lanes), lambda i: (0, 0, i)),
        out_shape=jax.ShapeDtypeStruct((t_fold, n_rows, d), MXU_DTYPE),
        scratch_shapes=[pltpu.VMEM((n_grp, n_rows, width), MXU_DTYPE), pltpu.VMEM((n_grp, n_rows, 2 * p2), F32)]
        + [pltpu.VMEM((S5_GROUP_BLOCK, n_rows, p2), MXU_DTYPE)] * 4,
        compiler_params=_cparams("parallel"),
        name="s5_scan_act",
    )(*([u2] * t_fold), d_skip.reshape(1, d), w_in, w_intra, w_out, scan_mul)
    return jnp.transpose(out, (1, 0, 2)).reshape(m, d)


def _chunk_index(d, c, n_ctx_chunks, n_chunks):
    back = jnp.where(c < n_ctx_chunks, n_ctx_chunks - 1 - c, n_chunks - 1 - (c - n_ctx_chunks))
    return jnp.where(d == 0, c, back)


def _tri_masks(chunk):
    lower = np.tril(np.ones((chunk, chunk), np.float32))
    return jnp.asarray(np.stack([np.stack([lower, lower.T]), np.stack([lower.T, lower])]))


def _split_hi_lo(x):
    hi = x.astype(jnp.bfloat16)
    lo = (x - hi.astype(F32)).astype(jnp.bfloat16)
    return hi, lo


def _gla_kernel(q_ref, k_ref, v_ref, r_ref, wg_ref, bg_ref, tri_ref, o_ref, s_ref, *, dk, dv, hb):
    d = pl.program_id(0)

    @pl.when(pl.program_id(2) == 0)
    def _():
        s_ref[...] = jnp.zeros_like(s_ref)

    w = hb * dk
    t = q_ref.shape[0]
    hh = t // 2
    x = jnp.dot(r_ref[...].astype(MXU_DTYPE), wg_ref[0].astype(MXU_DTYPE), preferred_element_type=F32) + bg_ref[0]
    la = _log_sigmoid(x) * (1.0 / GLA_TAU)
    tri = tri_ref[0, 0]
    vis = tri > 0
    hi, lo = _split_hi_lo(la)
    bb = jnp.dot(tri.astype(jnp.bfloat16), jnp.concatenate([hi, lo], axis=1), preferred_element_type=F32)
    b = bb[:, :w] + bb[:, w:]
    tot_a = jnp.sum(la[:hh], axis=0, keepdims=True)
    tot_b = jnp.sum(la[hh:], axis=0, keepdims=True)
    tot = tot_a + tot_b
    fwd = d == 0
    mid_a = jnp.where(fwd, 0.5 * tot_a, tot_b + 0.5 * tot_a)
    mid_b = jnp.where(fwd, tot_a + 0.5 * tot_b, 0.5 * tot_b)
    cross = jnp.where(fwd, tot_a, tot_b)
    ref = jnp.concatenate([jnp.broadcast_to(mid_a, (hh, w)), jnp.broadcast_to(mid_b, (t - hh, w))], axis=0)
    q = q_ref[...].astype(F32) * dk ** -0.5
    k = k_ref[...].astype(F32)
    qe = (q * jnp.exp(b - ref)).astype(MXU_DTYPE)
    ke = (k * jnp.exp(ref - b)).astype(MXU_DTYPE)
    qc = (q * jnp.exp(b - cross)).astype(MXU_DTYPE)
    kc = (k * jnp.exp(cross - b)).astype(MXU_DTYPE)
    qb = (q * jnp.exp(b)).astype(MXU_DTYPE)
    kd = (k * jnp.exp(tot - b)).astype(MXU_DTYPE)
    dec = jnp.exp(tot)
    nt = (((1,), (1,)), ((), ()))
    for h in range(hb):
        ks = slice(h * dk, (h + 1) * dk)
        vs = slice(h * dv, (h + 1) * dv)
        v = v_ref[:, vs].astype(MXU_DTYPE)
        blk = lambda qq, kk, rq, rk: lax.dot_general(qq[rq, ks], kk[rk, ks], nt, preferred_element_type=F32)
        ra, rb = slice(0, hh), slice(hh, t)
        att = jnp.concatenate([
            jnp.concatenate([blk(qe, ke, ra, ra), blk(qc, kc, ra, rb)], axis=1),
            jnp.concatenate([blk(qc, kc, rb, ra), blk(qe, ke, rb, rb)], axis=1)], axis=0)
        att = jnp.where(vis, att, 0.0).astype(MXU_DTYPE)
        st = s_ref[h]
        o_ref[0, :, vs] = (jnp.dot(att, v, preferred_element_type=F32)
                           + lax.dot_general(qb[:, ks], st.astype(MXU_DTYPE), nt, preferred_element_type=F32))
        upd = lax.dot_general(v, kd[:, ks], (((0,), (0,)), ((), ())), preferred_element_type=F32)
        s_ref[h] = st * dec[:, ks] + upd


def gla_scan(proj, r, w_gate_up, b_gate, ctx_len):
    m = proj.shape[0]
    n_dir, rank, qk = w_gate_up.shape
    dk = qk // HEADS
    dv = 2 * dk
    hb = HEAD_BLOCK
    nhb = HEADS // hb
    n_chunks = m // CHUNK
    n_ctx = ctx_len // CHUNK
    wg = jnp.zeros((n_dir, GATE_PAD, qk), F32)
    for dr in range(n_dir):
        wg = wg.at[dr, dr * rank:(dr + 1) * rank].set(w_gate_up[dr])
    cidx = functools.partial(_chunk_index, n_ctx_chunks=n_ctx, n_chunks=n_chunks)
    return pl.pallas_call(
        functools.partial(_gla_kernel, dk=dk, dv=dv, hb=hb),
        grid=(n_dir, nhb, n_chunks),
        in_specs=[
            pl.BlockSpec((CHUNK, hb * dk), lambda d, h, c: (cidx(d, c), h)),
            pl.BlockSpec((CHUNK, hb * dk), lambda d, h, c: (cidx(d, c), nhb + h)),
            pl.BlockSpec((CHUNK, hb * dv), lambda d, h, c: (cidx(d, c), nhb + h)),
            pl.BlockSpec((CHUNK, GATE_PAD), lambda d, h, c: (cidx(d, c), 0)),
            pl.BlockSpec((1, GATE_PAD, hb * dk), lambda d, h, c: (d, 0, h)),
            pl.BlockSpec((1, 1, hb * dk), lambda d, h, c: (d, 0, h)),
            pl.BlockSpec((1, 1, CHUNK, CHUNK), lambda d, h, c: (d, 0, 0, 0)),
        ],
        out_specs=pl.BlockSpec((1, CHUNK, hb * dv), lambda d, h, c: (d, cidx(d, c), h)),
        out_shape=jax.ShapeDtypeStruct((n_dir, m, dv * HEADS), F32),
        scratch_shapes=[pltpu.VMEM((hb, dv, dk), F32)],
        compiler_params=_cparams("arbitrary", "arbitrary", "arbitrary"),
        name="gla_scan",
    )(proj, proj, proj, r, wg, b_gate.reshape(n_dir, 1, qk), _tri_masks(CHUNK))


def _mlstm_kernel(q_ref, k_ref, v_ref, gcol_ref, grow_ref, tri_ref, o_ref, c_ref, m_ref, *, dk, dv, hb):
    d = pl.program_id(0)
    h0 = pl.program_id(1) * hb

    @pl.when(pl.program_id(2) == 0)
    def _():
        c_ref[...] = jnp.zeros_like(c_ref)
        m_ref[...] = jnp.zeros_like(m_ref)

    t = q_ref.shape[0]
    tri = tri_ref[0, 0]
    tri_t = tri_ref[0, 1]
    vis = tri > 0
    lane = lax.broadcasted_iota(jnp.int32, (1, gcol_ref.shape[1]), 1)
    gcol = gcol_ref[...]
    row = lax.broadcasted_iota(jnp.int32, (t, 1), 0)
    last = jnp.where(d == 0, t - 1, 0)
    e0 = (lax.broadcasted_iota(jnp.int32, (t, 128), 1) == 0).astype(MXU_DTYPE)
    for h in range(hb):
        ks = slice(h * dk, (h + 1) * dk)
        vs = slice(h * dv, (h + 1) * dv)
        col_i = 2 * d * HEADS + h0 + h
        li_c = jnp.sum(jnp.where(lane == col_i, gcol, 0.0), axis=1, keepdims=True)
        lf_c = _log_sigmoid(jnp.sum(jnp.where(lane == col_i + HEADS, gcol, 0.0), axis=1, keepdims=True))
        li_r = grow_ref[0, pl.ds(col_i, 1), :]
        lf_r = _log_sigmoid(grow_ref[0, pl.ds(col_i + HEADS, 1), :])
        b_c = jnp.sum(tri * lf_r, axis=1, keepdims=True)
        b_r = jnp.sum(tri_t * lf_c, axis=0, keepdims=True)
        b_last = jnp.sum(lf_r, axis=1, keepdims=True)
        m0 = m_ref[h][:, 0:1]
        logw = jnp.where(vis, b_c - b_r + li_r, -jnp.inf)
        w_inter = b_c + m0
        m_t = jnp.maximum(w_inter, jnp.max(logw, axis=1, keepdims=True))
        m_new = jnp.max(jnp.where(row == last, m_t, -jnp.inf), axis=0, keepdims=True)
        carry = jnp.exp(b_last + m0 - m_new)
        w_last_c = jnp.exp(b_last - b_c + li_c - m_new)
        q = q_ref[:, ks].astype(MXU_DTYPE)
        kf = k_ref[:, ks].astype(F32) * dk ** -0.5
        v_ext = jnp.concatenate([v_ref[:, vs].astype(MXU_DTYPE), e0], axis=1)
        c0 = c_ref[h]
        qk = lax.dot_general(q, kf.astype(MXU_DTYPE), (((1,), (1,)), ((), ())), preferred_element_type=F32)
        qk = qk * jnp.exp(logw - m_t)
        inter = jnp.exp(w_inter - m_t)
        num = (jnp.dot(qk.astype(MXU_DTYPE), v_ext, preferred_element_type=F32)
               + inter * jnp.dot(q, c0.astype(MXU_DTYPE), preferred_element_type=F32))
        den = num[:, dv:dv + 1]
        o_ref[0, :, vs] = num[:, :dv] / jnp.maximum(jnp.abs(den), jnp.exp(-m_t))
        kw = (kf * w_last_c).astype(MXU_DTYPE)
        c_ref[h] = carry * c0 + lax.dot_general(kw, v_ext, (((0,), (0,)), ((), ())), preferred_element_type=F32)
        m_ref[h] = jnp.broadcast_to(m_new, m_ref.shape[1:])


def mlstm_scan(proj, gates, ctx_len):
    m = proj.shape[0]
    dk = proj.shape[1] // (6 * HEADS)
    dv = 2 * dk
    hb = HEAD_BLOCK
    nhb = HEADS // hb
    n_chunks = m // ML_CHUNK
    n_ctx = ctx_len // ML_CHUNK
    n_gate = 4 * HEADS
    grow = jnp.transpose(gates[:, :n_gate].reshape(n_chunks, ML_CHUNK, n_gate), (0, 2, 1))
    cidx = functools.partial(_chunk_index, n_ctx_chunks=n_ctx, n_chunks=n_chunks)
    return pl.pallas_call(
        functools.partial(_mlstm_kernel, dk=dk, dv=dv, hb=hb),
        grid=(2, nhb, n_chunks),
        in_specs=[
            pl.BlockSpec((ML_CHUNK, hb * dk), lambda d, h, c: (cidx(d, c), h)),
            pl.BlockSpec((ML_CHUNK, hb * dk), lambda d, h, c: (cidx(d, c), nhb + h)),
            pl.BlockSpec((ML_CHUNK, hb * dv), lambda d, h, c: (cidx(d, c), nhb + h)),
            pl.BlockSpec((ML_CHUNK, GATE_PAD), lambda d, h, c: (cidx(d, c), 0)),
            pl.BlockSpec((1, n_gate, ML_CHUNK), lambda d, h, c: (cidx(d, c), 0, 0)),
            pl.BlockSpec((1, 2, ML_CHUNK, ML_CHUNK), lambda d, h, c: (d, 0, 0, 0)),
        ],
        out_specs=pl.BlockSpec((1, ML_CHUNK, hb * dv), lambda d, h, c: (d, cidx(d, c), h)),
        out_shape=jax.ShapeDtypeStruct((2, m, dv * HEADS), F32),
        scratch_shapes=[pltpu.VMEM((hb, dk, dv + 128), F32), pltpu.VMEM((hb, 1, 128), F32)],
        compiler_params=_cparams("arbitrary", "arbitrary", "arbitrary"),
        name="mlstm_scan",
    )(proj, proj, proj, gates, grow, _tri_masks(ML_CHUNK))


def _pad_cols(w, width):
    return jnp.pad(w, ((0, 0), (0, width - w.shape[1])))


def _s5_layer(u, p, j, ctx_len):
    weights = _s5_weights(p['s5_lam_re'][j], p['s5_lam_im'][j], p['s5_log_step'][j], p['s5_b_re'][j],
                          p['s5_b_im'][j], p['s5_c_re'][j], p['s5_c_im'][j])
    act = s5_scan_act(u, p['s5_d'][j], weights, ctx_len)
    return matmul(act, p['s5_w_glu'][j].astype(MXU_DTYPE), p['s5_b_glu'][j], MXU_DTYPE)


def _gla_layer(u, p, j, ctx_len):
    w_in = p['gla_w_in'][j]
    n_main = w_in.shape[1] - 2 * p['gla_w_gate_up'].shape[2]
    proj = matmul(u, cast_cols(w_in, n_main, MXU_DTYPE), jnp.zeros((n_main,), F32), MXU_DTYPE)
    r = matmul(u, _pad_cols(w_in[:, n_main:], GATE_PAD).astype(MXU_DTYPE), jnp.zeros((GATE_PAD,), F32), F32)
    o2 = gla_scan(proj, r, p['gla_w_gate_up'][j], p['gla_b_gate'][j], ctx_len)
    on = head_gate(o2, proj, 2, p['gla_norm_g'][j], sigmoid_gate=False)
    d = u.shape[1]
    return matmul(on, p['gla_w_out'][j].astype(MXU_DTYPE), jnp.zeros((d,), F32), MXU_DTYPE)


def _mlstm_layer(u, p, j, ctx_len):
    w_in = p['mlstm_w_in'][j]
    n_gate = 4 * HEADS
    n_main = w_in.shape[1] - n_gate
    proj = matmul(u, cast_cols(w_in, n_main, MXU_DTYPE), jnp.zeros((n_main,), F32), MXU_DTYPE)
    b_gate = jnp.pad(p['mlstm_b_gate'][j].reshape(n_gate), (0, GATE_PAD - n_gate))
    gates = matmul(u, _pad_cols(w_in[:, n_main:], GATE_PAD).astype(MXU_DTYPE), b_gate, F32)
    h2 = mlstm_scan(proj, gates, ctx_len)
    hn = head_gate(h2, proj, 2, p['mlstm_norm_g'][j], sigmoid_gate=True)
    d = u.shape[1]
    return matmul(hn, p['mlstm_w_out'][j].astype(MXU_DTYPE), jnp.zeros((d,), F32), MXU_DTYPE)


def kernel(x, c, ctx, c_ctx, ada_w_down, ada_w_up, ada_b, norm_g, ffn_w_in, ffn_conv_w, ffn_conv_b, ffn_w_out, s5_lam_re, s5_lam_im, s5_log_step, s5_b_re, s5_b_im, s5_c_re, s5_c_im, s5_d, s5_w_glu, s5_b_glu, gla_w_in, gla_w_gate_up, gla_b_gate, gla_norm_g, gla_w_out, mlstm_w_in, mlstm_b_gate, mlstm_norm_g, mlstm_w_out):
    p = dict(s5_lam_re=s5_lam_re, s5_lam_im=s5_lam_im, s5_log_step=s5_log_step, s5_b_re=s5_b_re, s5_b_im=s5_b_im,
             s5_c_re=s5_c_re, s5_c_im=s5_c_im, s5_d=s5_d, s5_w_glu=s5_w_glu, s5_b_glu=s5_b_glu,
             gla_w_in=gla_w_in, gla_w_gate_up=gla_w_gate_up, gla_b_gate=gla_b_gate, gla_norm_g=gla_norm_g,
             gla_w_out=gla_w_out, mlstm_w_in=mlstm_w_in, mlstm_b_gate=mlstm_b_gate, mlstm_norm_g=mlstm_norm_g,
             mlstm_w_out=mlstm_w_out)
    bsz, seq, d = x.shape
    ctx_len = ctx.shape[1]
    depth = ada_w_down.shape[0]
    assert bsz == 1 and ctx_len == ROW_TILE and seq % (GRID_W * (ROW_TILE // GRID_W)) == 0
    cond = jnp.zeros((8, d), F32).at[0].set(c_ctx).at[1].set(c[0])
    mods = ada_all(cond, ada_w_down, ada_w_up, ada_b)[:, :2].reshape(depth, 2, 6, 1, d)
    mod = lambda i, n: mods[i, :, n]
    mixer_dtype = lambda i: F32 if i % 3 == 0 else MXU_DTYPE
    h = jnp.concatenate([ctx[0], x[0]], axis=0)
    u = norm_mod(h, norm_g[0, 0], mod(0, 1), mod(0, 0), ctx_len, mixer_dtype(0))
    f2 = ffn_w_in.shape[2]
    for i in range(depth):
        kind, j = i % 3, i // 3
        ffn_in = (norm_g[i, 2], mod(i, 4), mod(i, 3), MXU_DTYPE)
        if kind == 0:
            h, uf = resid(h, _s5_layer(u, p, j, ctx_len), norm_g[i, 1], mod(i, 2), ctx_len, glu=True, nxt=ffn_in)
        else:
            y = _gla_layer(u, p, j, ctx_len) if kind == 1 else _mlstm_layer(u, p, j, ctx_len)
            h, uf = resid(h, y, norm_g[i, 1], mod(i, 2), ctx_len, nxt=ffn_in)
        gu = matmul(uf, ffn_w_in[i].astype(MXU_DTYPE), jnp.zeros((f2,), F32), MXU_DTYPE)
        act = conv_act(gu, ffn_conv_w[i], ffn_conv_b[i], ctx_len)
        f = matmul(act, ffn_w_out[i].astype(MXU_DTYPE), jnp.zeros((d,), F32), MXU_DTYPE)
        if i + 1 < depth:
            nxt = (norm_g[i + 1, 0], mod(i + 1, 1), mod(i + 1, 0), mixer_dtype(i + 1))
            h, u = resid(h, f, norm_g[i, 3], mod(i, 5), ctx_len, nxt=nxt)
        else:
            h = resid(h, f, norm_g[i, 3], mod(i, 5), ctx_len, drop_ctx=True)
    return h.reshape(bsz, seq, d)
```

```python
import functools
import math

import numpy as np
import jax
import jax.numpy as jnp
from jax import lax
from jax.experimental import pallas as pl
from jax.experimental.pallas import tpu as pltpu

EPS = 1e-6
GRID_W = 64
S5_GROUP = 16
S5_T = 16
S5_GROUP_BLOCK = 4
HEADS = 8
HEAD_BLOCK = 4
CHUNK = 128
ML_CHUNK = 128
GLA_TAU = 16.0
GATE_PAD = 128
ROW_TILE = 256
SUBLANES = 8
MXU_DTYPE = jnp.bfloat16
V7X_VMEM_LIMIT_BYTES = 48 * 1024 * 1024

F32 = jnp.float32


def _cparams(*sem):
    return pltpu.CompilerParams(dimension_semantics=sem, vmem_limit_bytes=V7X_VMEM_LIMIT_BYTES)


def _pick(dim, candidates):
    for c in candidates:
        if dim % c == 0:
            return c
    raise ValueError(f"no tile for {dim} in {candidates}")


def _log_sigmoid(x):
    return jnp.minimum(x, 0.0) - jnp.log1p(jnp.exp(-jnp.abs(x)))


def _sigmoid(x):
    return 1.0 / (1.0 + jnp.exp(-x))


def _silu(x):
    return x * _sigmoid(x)


def _gelu_tanh(x):
    return 0.5 * x * (1.0 + jnp.tanh(math.sqrt(2.0 / math.pi) * (x + 0.044715 * (x * x * x))))


def _ada_kernel(cond_ref, wd_ref, wu_ref, b_ref, o_ref):
    s = _silu(cond_ref[...])
    t = jnp.dot(s, wd_ref[0], preferred_element_type=F32, precision=lax.Precision.HIGHEST)
    o_ref[0] = jnp.dot(t, wu_ref[0], preferred_element_type=F32, precision=lax.Precision.HIGHEST) + b_ref[0]


def ada_all(cond, w_down, w_up, b):
    depth, d, rank = w_down.shape
    n = w_up.shape[2]
    tn = _pick(n, (2048, 1024, 512, 256, 128))
    return pl.pallas_call(
        _ada_kernel,
        grid=(depth, n // tn),
        in_specs=[
            pl.BlockSpec((8, d), lambda l, j: (0, 0)),
            pl.BlockSpec((1, d, rank), lambda l, j: (l, 0, 0)),
            pl.BlockSpec((1, rank, tn), lambda l, j: (l, 0, j)),
            pl.BlockSpec((1, 1, tn), lambda l, j: (l, 0, j)),
        ],
        out_specs=pl.BlockSpec((1, 8, tn), lambda l, j: (l, 0, j)),
        out_shape=jax.ShapeDtypeStruct((depth, 8, n), F32),
        compiler_params=_cparams("arbitrary", "arbitrary"),
        name="ada_all",
    )(cond, w_down, w_up, b.reshape(depth, 1, n))


def _rms(x, g):
    return x * lax.rsqrt(jnp.mean(x * x, axis=-1, keepdims=True) + EPS) * g


def _norm_mod_kernel(x_ref, g_ref, sc_ref, sh_ref, o_ref):
    y = _rms(x_ref[...], g_ref[...])
    o_ref[...] = (y * (1.0 + sc_ref[0]) + sh_ref[0]).astype(o_ref.dtype)


def _row_specs(d, ctx_tiles, row_offset=0):
    row = pl.BlockSpec((ROW_TILE, d), lambda i: (i + row_offset, 0))
    vec = pl.BlockSpec((1, d), lambda i: (0, 0))
    mod = pl.BlockSpec((1, 1, d), lambda i: (jnp.where(i + row_offset < ctx_tiles, 0, 1), 0, 0))
    return row, vec, mod


def norm_mod(x, g, scale, shift, ctx_len, out_dtype):
    m, d = x.shape
    row, vec, mod = _row_specs(d, ctx_len // ROW_TILE)
    return pl.pallas_call(
        _norm_mod_kernel,
        grid=(m // ROW_TILE,),
        in_specs=[row, vec, mod, mod],
        out_specs=row,
        out_shape=jax.ShapeDtypeStruct((m, d), out_dtype),
        compiler_params=_cparams("parallel"),
        name="norm_mod",
    )(x, g.reshape(1, d), scale, shift)


def _resid_kernel(*refs, glu, has_next):
    h_ref, y_ref = refs[0], refs[1]
    n_in = 2 + (1 if glu else 0)
    g_ref, gate_ref = refs[n_in], refs[n_in + 1]
    y = y_ref[...].astype(F32)
    if glu:
        y = y * _sigmoid(refs[2][...].astype(F32))
    h = h_ref[...] + gate_ref[0] * _rms(y, g_ref[...])
    if has_next:
        gn_ref, sc_ref, sh_ref, o_ref, u_ref = refs[n_in + 2:]
        u_ref[...] = (_rms(h, gn_ref[...]) * (1.0 + sc_ref[0]) + sh_ref[0]).astype(u_ref.dtype)
    else:
        o_ref = refs[n_in + 2]
    o_ref[...] = h


def resid(h, y, g, gate, ctx_len, *, glu=False, nxt=None, drop_ctx=False):
    m, d = h.shape
    ctx_tiles = ctx_len // ROW_TILE
    off = ctx_tiles if drop_ctx else 0
    row, vec, mod = _row_specs(d, ctx_tiles, off)
    out_row = pl.BlockSpec((ROW_TILE, d), lambda i: (i, 0))
    m_out = m - off * ROW_TILE
    ins, in_specs = [h, y], [row, row]
    if glu:
        ins.append(y)
        in_specs.append(pl.BlockSpec((ROW_TILE, d), lambda i: (i + off, 1)))
    ins += [g.reshape(1, d), gate]
    in_specs += [vec, mod]
    out_shape = [jax.ShapeDtypeStruct((m_out, d), F32)]
    out_specs = [out_row]
    if nxt is not None:
        g_next, scale, shift, dtype = nxt
        ins += [g_next.reshape(1, d), scale, shift]
        in_specs += [vec, mod, mod]
        out_shape.append(jax.ShapeDtypeStruct((m_out, d), dtype))
        out_specs.append(out_row)
    res = pl.pallas_call(
        functools.partial(_resid_kernel, glu=glu, has_next=nxt is not None),
        grid=(m_out // ROW_TILE,),
        in_specs=in_specs,
        out_specs=out_specs,
        out_shape=out_shape,
        compiler_params=_cparams("parallel"),
        name="resid",
    )(*ins)
    return res if nxt is not None else res[0]


def _head_gate_kernel(o_ref, gate_ref, g_ref, out_ref, *, heads, sigmoid_gate):
    o = o_ref[0].astype(F32) + o_ref[1].astype(F32)
    dv = o.shape[1] // heads
    for h in range(heads):
        sl = slice(h * dv, (h + 1) * dv)
        gt = gate_ref[:, sl].astype(F32)
        act = _sigmoid(gt) if sigmoid_gate else _silu(gt)
        out_ref[:, sl] = (_rms(o[:, sl], g_ref[...]) * act).astype(out_ref.dtype)


def head_gate(o2, proj, gate_col_block, norm_g, sigmoid_gate):
    _, m, d = o2.shape
    dv = d // HEADS
    return pl.pallas_call(
        functools.partial(_head_gate_kernel, heads=HEADS, sigmoid_gate=sigmoid_gate),
        grid=(m // ROW_TILE,),
        in_specs=[
            pl.BlockSpec((2, ROW_TILE, d), lambda i: (0, i, 0)),
            pl.BlockSpec((ROW_TILE, d), lambda i: (i, gate_col_block)),
            pl.BlockSpec((1, dv), lambda i: (0, 0)),
        ],
        out_specs=pl.BlockSpec((ROW_TILE, d), lambda i: (i, 0)),
        out_shape=jax.ShapeDtypeStruct((m, d), MXU_DTYPE),
        compiler_params=_cparams("parallel"),
        name="head_gate",
    )(o2, proj, norm_g.reshape(1, dv))


def _mm_kernel(a_ref, w_ref, b_ref, o_ref, acc_ref, *, nk):
    k = pl.program_id(2)

    @pl.when(k == 0)
    def _():
        acc_ref[...] = jnp.zeros_like(acc_ref)

    acc_ref[...] += jnp.dot(a_ref[...].astype(MXU_DTYPE), w_ref[...], preferred_element_type=F32)

    @pl.when(k == nk - 1)
    def _():
        o_ref[...] = (acc_ref[...] + b_ref[...]).astype(o_ref.dtype)


def _mm_full_k_kernel(a_ref, w_ref, b_ref, o_ref):
    acc = jnp.dot(a_ref[...].astype(MXU_DTYPE), w_ref[...], preferred_element_type=F32)
    o_ref[...] = (acc + b_ref[...]).astype(o_ref.dtype)


MM_FULL_K_MAX = 4096


def matmul(a, w, bias, out_dtype):
    m, k = a.shape
    n = w.shape[1]
    tm = _pick(m, (1280, 1024, 640, 512, 256))
    tn = _pick(n, (1024, 512, 256, 128))
    bias2 = bias.reshape(1, n).astype(F32)
    if k <= MM_FULL_K_MAX:
        return pl.pallas_call(
            _mm_full_k_kernel,
            grid=(m // tm, n // tn),
            in_specs=[
                pl.BlockSpec((tm, k), lambda i, j: (i, 0), pipeline_mode=pl.Buffered(1)),
                pl.BlockSpec((k, tn), lambda i, j: (0, j)),
                pl.BlockSpec((1, tn), lambda i, j: (0, j)),
            ],
            out_specs=pl.BlockSpec((tm, tn), lambda i, j: (i, j)),
            out_shape=jax.ShapeDtypeStruct((m, n), out_dtype),
            compiler_params=_cparams("parallel", "arbitrary"),
            name="matmul_full_k",
        )(a, w, bias2)
    tk = _pick(k, (2048, 1024, 512, 256, 128))
    nk = k // tk
    return pl.pallas_call(
        functools.partial(_mm_kernel, nk=nk),
        grid=(m // tm, n // tn, nk),
        in_specs=[
            pl.BlockSpec((tm, tk), lambda i, j, kk: (i, kk)),
            pl.BlockSpec((tk, tn), lambda i, j, kk: (kk, j)),
            pl.BlockSpec((1, tn), lambda i, j, kk: (0, j)),
        ],
        out_specs=pl.BlockSpec((tm, tn), lambda i, j, kk: (i, j)),
        out_shape=jax.ShapeDtypeStruct((m, n), out_dtype),
        scratch_shapes=[pltpu.VMEM((tm, tn), F32)],
        compiler_params=_cparams("parallel", "parallel", "arbitrary"),
        name="matmul",
    )(a, w, bias2)


def _cast_kernel(x_ref, o_ref):
    o_ref[...] = x_ref[0].astype(o_ref.dtype)


def cast_cols(w, layer, n_cols, dtype):
    k = w.shape[1]
    tk = _pick(k, (512, 256, 128))
    tn = _pick(n_cols, (2048, 1024, 512, 256, 128))
    return pl.pallas_call(
        _cast_kernel,
        grid=(k // tk, n_cols // tn),
        in_specs=[pl.BlockSpec((1, tk, tn), lambda i, j: (layer, i, j))],
        out_specs=pl.BlockSpec((tk, tn), lambda i, j: (i, j)),
        out_shape=jax.ShapeDtypeStruct((k, n_cols), dtype),
        compiler_params=_cparams("parallel", "parallel"),
        name="cast_cols",
    )(w)


def _conv_act_kernel(gp_ref, gc_ref, gn_ref, up_ref, w_ref, b_ref, sh_ref, o_ref, *, tb, ctx_tiles, grid_w):
    i = pl.program_id(0)
    n_tiles = pl.num_programs(0)
    up = up_ref[...].astype(F32)

    def finish(conv):
        o_ref[...] = (_silu(conv + b_ref[...]) * up).astype(o_ref.dtype)

    def tap(dr, dc):
        k = 3 * (dr + 1) + (dc + 1)
        return w_ref[k:k + 1, :]

    @pl.when(i < ctx_tiles)
    def _():
        g = gc_ref[...].astype(F32)
        pos = lax.broadcasted_iota(jnp.int32, (tb, 1), 0)
        left = jnp.where(pos > 0, pltpu.roll(g, 1, axis=0), 0.0)
        right = jnp.where(pos < tb - 1, pltpu.roll(g, tb - 1, axis=0), 0.0)
        finish(tap(0, -1) * left + tap(0, 0) * g + tap(0, 1) * right)

    @pl.when(i >= ctx_tiles)
    def _():
        zero = jnp.zeros_like(gp_ref[...])
        blocks = ([jnp.where(i > ctx_tiles, gp_ref[...], zero)]
                  + [gc_ref[r * grid_w:(r + 1) * grid_w, :] for r in range(tb // grid_w)]
                  + [jnp.where(i < n_tiles - 1, gn_ref[...], zero)])
        shifts = sh_ref[...]
        src = []
        for blk in blocks:
            lr = jnp.dot(shifts, blk, preferred_element_type=F32)
            src.append({-1: lr[:grid_w], 0: blk.astype(F32), 1: lr[grid_w:]})
        rows = []
        for r in range(tb // grid_w):
            conv = None
            for dr in (-1, 0, 1):
                for dc in (-1, 0, 1):
                    term = tap(dr, dc) * src[r + 1 + dr][dc]
                    conv = term if conv is None else conv + term
            rows.append(conv)
        finish(jnp.concatenate(rows, axis=0))


def _shift_mats(n, dtype):
    left = np.eye(n, k=-1, dtype=np.float32)
    return jnp.asarray(np.concatenate([left, left.T], axis=0), dtype)


def conv_act(gu, conv_w, conv_b, ctx_len):
    m, f2 = gu.shape
    f = f2 // 2
    tb = ROW_TILE
    assert GRID_W & (GRID_W - 1) == 0 and tb % GRID_W == 0 and ctx_len == tb
    tc = _pick(f, (1024, 512, 256, 128))
    nblk = m // GRID_W
    r = tb // GRID_W
    kern = functools.partial(_conv_act_kernel, tb=tb, ctx_tiles=ctx_len // tb, grid_w=GRID_W)
    return pl.pallas_call(
        kern,
        grid=(m // tb, f // tc),
        in_specs=[
            pl.BlockSpec((GRID_W, tc), lambda i, j: (jnp.maximum(i * r - 1, 0), j)),
            pl.BlockSpec((tb, tc), lambda i, j: (i, j)),
            pl.BlockSpec((GRID_W, tc), lambda i, j: (jnp.minimum((i + 1) * r, nblk - 1), j)),
            pl.BlockSpec((tb, tc), lambda i, j: (i, j + f // tc)),
            pl.BlockSpec((9, tc), lambda i, j: (0, j)),
            pl.BlockSpec((1, tc), lambda i, j: (0, j)),
            pl.BlockSpec((2 * GRID_W, GRID_W), lambda i, j: (0, 0)),
        ],
        out_specs=pl.BlockSpec((tb, tc), lambda i, j: (i, j)),
        out_shape=jax.ShapeDtypeStruct((m, f), MXU_DTYPE),
        compiler_params=_cparams("parallel", "parallel"),
        name="conv_act",
    )(gu, gu, gu, gu, conv_w.reshape(9, f), conv_b.reshape(1, f), _shift_mats(GRID_W, gu.dtype))


def _s5_weights(lam_re, lam_im, log_step, b_re, b_im, c_re, c_im):
    t = S5_T
    lam = lax.complex(lam_re.astype(F32), lam_im.astype(F32))
    step = jnp.exp(log_step.astype(F32))[:, :, None]
    pw = jnp.exp(jnp.arange(t + 1, dtype=F32)[:, None, None, None] * (lam * step)[None])
    lam_bar = pw[1]
    b_bar = ((lam_bar - 1.0) / lam)[..., None] * lax.complex(b_re.astype(F32), b_im.astype(F32))
    c_mat = lax.complex(c_re.astype(F32), c_im.astype(F32))
    g_n, p_n, i_n = b_bar.shape[1], b_bar.shape[2], b_bar.shape[3]
    kern = jnp.real(jnp.einsum('dgip,tdgp,dgpj->tdgij', c_mat, pw[:t], b_bar,
                               precision=lax.Precision.HIGHEST))
    kf = jnp.transpose(kern[:, 0], (1, 3, 0, 2))
    kb = jnp.transpose(kern[::-1, 1], (1, 3, 0, 2))
    zero = jnp.zeros_like(kf[:, :, :1])
    lags = jnp.concatenate([zero, kb, zero.repeat(t - 1, axis=2)], axis=2) \
        + jnp.concatenate([zero.repeat(t, axis=2), kf], axis=2)
    period = jnp.concatenate([lags, zero], axis=2).reshape(g_n, i_n, (2 * t + 1) * i_n)
    skew = jnp.tile(period, (1, 1, t))[:, :, :t * 2 * t * i_n].reshape(g_n, i_n, t, 2 * t, i_n)
    w_intra = jnp.transpose(skew[:, :, :, t:], (0, 2, 1, 3, 4)).reshape(g_n, t * i_n, t * i_n)
    vf = pw[t - 1 - np.arange(t), 0][..., None] * b_bar[0][None]
    vb = pw[np.arange(t), 1][..., None] * b_bar[1][None]
    w_in = jnp.concatenate([jnp.real(vf), jnp.real(vb), jnp.imag(vf), jnp.imag(vb)], axis=2)
    w_in = jnp.transpose(w_in, (1, 0, 3, 2)).reshape(g_n, t * i_n, 4 * p_n)
    qf = c_mat[0][None] * pw[1 + np.arange(t), 0][:, :, None, :]
    qb = c_mat[1][None] * pw[t - np.arange(t), 1][:, :, None, :]
    w_out = jnp.concatenate([jnp.real(qf), jnp.real(qb), -jnp.imag(qf), -jnp.imag(qb)], axis=3)
    w_out = jnp.transpose(w_out, (1, 3, 0, 2)).reshape(g_n, 4 * p_n, t * i_n)
    sl = SUBLANES
    apow = jnp.exp(((1 + jnp.arange(sl, dtype=F32)) * t)[:, None, None, None] * (lam * step)[None])
    rows = np.arange(sl)
    both = lambda f, b: jnp.concatenate([f, b], axis=-1)
    planes = []
    for dist in (1, 2, 4):
        mf = apow[dist - 1, 0][:, None, :] * jnp.asarray(rows >= dist, F32)[None, :, None]
        mb = apow[dist - 1, 1][:, None, :] * jnp.asarray(rows <= sl - 1 - dist, F32)[None, :, None]
        planes.append(both(mf, mb))
    planes.append(both(jnp.transpose(apow[rows, 0], (1, 0, 2)), jnp.transpose(apow[sl - 1 - rows, 1], (1, 0, 2))))
    scan_mul = jnp.stack([f(pl_) for pl_ in planes for f in (jnp.real, jnp.imag)], axis=1)
    return (w_in.astype(MXU_DTYPE), w_intra.astype(MXU_DTYPE), w_out.astype(MXU_DTYPE), scan_mul)


def _block_transpose(arrs, lane, width):
    n = len(arrs)
    lanes = n * width
    dist = n // 2
    while dist >= 1:
        w = dist * width
        low = jnp.bitwise_and(lane, 2 * w - 1) < w
        nxt = list(arrs)
        for a in range(n):
            if (a // dist) % 2 == 0:
                b = a + dist
                nxt[a] = jnp.where(low, arrs[a], pltpu.roll(arrs[b], w, axis=1))
                nxt[b] = jnp.where(low, pltpu.roll(arrs[a], lanes - w, axis=1), arrs[b])
        arrs = nxt
        dist //= 2
    return arrs


def _s5_kernel(*refs, n_rows, n_ctx_rows, p2, t_fold, grp, scan_groups, row_chunk):
    x_refs = refs[:t_fold]
    d_ref, win_ref, wintra_ref, wout_ref, mul_ref, o_ref = refs[t_fold:t_fold + 6]
    ug_ref, zy_ref, hrf_ref, hrb_ref, hif_ref, hib_ref = refs[t_fold + 6:]
    lanes = x_refs[0].shape[1]
    n_grp = lanes // grp
    slots = lanes // grp
    n_half = t_fold // slots
    half = p2 // 2
    lane = lax.broadcasted_iota(jnp.int32, (1, lanes), 1)
    n_chunks = n_rows // row_chunk
    sub = 16

    def gather(rc, _):
        rows = pl.ds(pl.multiple_of(rc * row_chunk, sub), row_chunk)
        halves = [_block_transpose([x_refs[hf * slots + s8][rows, :] for s8 in range(slots)], lane, grp)
                  for hf in range(n_half)]
        for g in range(n_grp):
            ug_ref[g, rows, :] = jnp.concatenate([hv[g] for hv in halves], axis=1).astype(ug_ref.dtype)
        return 0

    lax.fori_loop(0, n_chunks, gather, 0)

    is_f = lax.broadcasted_iota(jnp.int32, (1, p2), 1) < half
    sl = SUBLANES
    tile_row = lax.broadcasted_iota(jnp.int32, (sl, 1), 0)
    first_row = tile_row == jnp.where(is_f, 0, sl - 1)
    n_tiles = n_rows // sub
    n_ctx_tiles = n_ctx_rows // sub

    for blk in range(n_grp // scan_groups):
        gs = [blk * scan_groups + k for k in range(scan_groups)]
        for g in gs:
            zy_ref[g] = jnp.dot(ug_ref[g], win_ref[g], preferred_element_type=F32)

        def step(i, carry, gs=gs):
            tf = pl.multiple_of(i * sub, sub)
            tb = pl.multiple_of(
                jnp.where(i < n_ctx_tiles, n_ctx_tiles - 1 - i, n_tiles - 1 - (i - n_ctx_tiles)) * sub, sub)
            out = []
            for k, g in enumerate(gs):
                c_r, c_i = carry[2 * k], carry[2 * k + 1]
                zrf, zrb = zy_ref[g, pl.ds(tf, sub), 0:p2], zy_ref[g, pl.ds(tb, sub), 0:p2]
                zif, zib = zy_ref[g, pl.ds(tf, sub), p2:2 * p2], zy_ref[g, pl.ds(tb, sub), p2:2 * p2]
                prev_r, prev_i = [], []
                for q in range(sub // sl):
                    fs = slice(q * sl, (q + 1) * sl)
                    bs = slice(sub - (q + 1) * sl, sub - q * sl)
                    v_r = jnp.where(is_f, zrf[fs], zrb[bs])
                    v_i = jnp.where(is_f, zif[fs], zib[bs])
                    for j, dist in enumerate((1, 2, 4)):
                        m_r, m_i = mul_ref[g, 2 * j], mul_ref[g, 2 * j + 1]
                        s_r = jnp.where(is_f, pltpu.roll(v_r, dist, axis=0), pltpu.roll(v_r, sl - dist, axis=0))
                        s_i = jnp.where(is_f, pltpu.roll(v_i, dist, axis=0), pltpu.roll(v_i, sl - dist, axis=0))
                        v_r, v_i = v_r + (m_r * s_r - m_i * s_i), v_i + (m_r * s_i + m_i * s_r)
                    p_r, p_i = mul_ref[g, 6], mul_ref[g, 7]
                    h_r = v_r + (p_r * c_r - p_i * c_i)
                    h_i = v_i + (p_r * c_i + p_i * c_r)
                    e_r = jnp.where(is_f, pltpu.roll(h_r, 1, axis=0), pltpu.roll(h_r, sl - 1, axis=0))
                    e_i = jnp.where(is_f, pltpu.roll(h_i, 1, axis=0), pltpu.roll(h_i, sl - 1, axis=0))
                    prev_r.append(jnp.where(first_row, c_r, e_r))
                    prev_i.append(jnp.where(first_row, c_i, e_i))
                    c_r = jnp.where(is_f, h_r[sl - 1:sl], h_r[0:1])
                    c_i = jnp.where(is_f, h_i[sl - 1:sl], h_i[0:1])
                hrf_ref[k, pl.ds(tf, sub), :] = jnp.concatenate(prev_r, axis=0).astype(hrf_ref.dtype)
                hrb_ref[k, pl.ds(tb, sub), :] = jnp.concatenate(prev_r[::-1], axis=0).astype(hrb_ref.dtype)
                hif_ref[k, pl.ds(tf, sub), :] = jnp.concatenate(prev_i, axis=0).astype(hif_ref.dtype)
                hib_ref[k, pl.ds(tb, sub), :] = jnp.concatenate(prev_i[::-1], axis=0).astype(hib_ref.dtype)
                out += [c_r, c_i]
            return tuple(out)

        zero = jnp.zeros((1, p2), F32)
        lax.fori_loop(0, n_tiles, step, tuple(zero for _ in range(2 * scan_groups)))
        for k, g in enumerate(gs):
            h_r = jnp.where(is_f, hrf_ref[k], hrb_ref[k])
            h_i = jnp.where(is_f, hif_ref[k], hib_ref[k])
            hp = jnp.concatenate([h_r, h_i], axis=1)
            zy_ref[g] = (jnp.dot(ug_ref[g], wintra_ref[g], preferred_element_type=F32)
                         + jnp.dot(hp, wout_ref[g], preferred_element_type=F32))

    def scatter(rc, _):
        rows = pl.ds(pl.multiple_of(rc * row_chunk, sub), row_chunk)
        for hf in range(n_half):
            ys = _block_transpose([zy_ref[g, rows, hf * lanes:(hf + 1) * lanes] for g in range(n_grp)], lane, grp)
            for t8 in range(slots):
                t = hf * slots + t8
                o_ref[t, rows, :] = _gelu_tanh(d_ref[...] * x_refs[t][rows, :] + ys[t8]).astype(o_ref.dtype)
        return 0

    lax.fori_loop(0, n_chunks, scatter, 0)


def s5_scan_act(u, d_skip, weights, ctx_len):
    m, d = u.shape
    w_in, w_intra, w_out, scan_mul = weights
    t_fold, grp, lanes = S5_T, S5_GROUP, 128
    n_rows = m // t_fold
    width = t_fold * grp
    p2 = scan_mul.shape[-1]
    n_grp = lanes // grp
    n_tile = d // lanes
    row_chunk = _pick(n_rows, (80, 16))
    assert n_rows % 16 == 0 and (ctx_len // t_fold) % 16 == 0 and t_fold % n_grp == 0
    kern = functools.partial(_s5_kernel, n_rows=n_rows, n_ctx_rows=ctx_len // t_fold, p2=p2, t_fold=t_fold,
                             grp=grp, scan_groups=S5_GROUP_BLOCK, row_chunk=row_chunk)
    blk = lambda a, b: pl.BlockSpec((n_grp, a, b), lambda i: (i, 0, 0))
    x_specs = [pl.BlockSpec((n_rows, lanes), functools.partial(lambda i, s: (0, s * n_tile + i), s=s),
                            pipeline_mode=pl.Buffered(1)) for s in range(t_fold)]
    u2 = u.reshape(n_rows, t_fold * d)
    out = pl.pallas_call(
        kern,
        grid=(n_tile,),
        in_specs=x_specs + [pl.BlockSpec((1, lanes), lambda i: (0, i)),
                            blk(width, 2 * p2), blk(width, width), blk(2 * p2, width),
                            pl.BlockSpec((n_grp,) + scan_mul.shape[1:], lambda i: (i, 0, 0, 0))],
        out_specs=pl.BlockSpec((t_fold, n_rows, lanes), lambda i: (0, 0, i)),
        out_shape=jax.ShapeDtypeStruct((t_fold, n_rows, d), MXU_DTYPE),
        scratch_shapes=[pltpu.VMEM((n_grp, n_rows, width), MXU_DTYPE), pltpu.VMEM((n_grp, n_rows, 2 * p2), F32)]
        + [pltpu.VMEM((S5_GROUP_BLOCK, n_rows, p2), MXU_DTYPE)] * 4,
        compiler_params=_cparams("parallel"),
        name="s5_scan_act",
    )(*([u2] * t_fold), d_skip.reshape(1, d), w_in, w_intra, w_out, scan_mul)
    return jnp.transpose(out, (1, 0, 2)).reshape(m, d)


def _chunk_index(d, c, n_ctx_chunks, n_chunks):
    back = jnp.where(c < n_ctx_chunks, n_ctx_chunks - 1 - c, n_chunks - 1 - (c - n_ctx_chunks))
    return jnp.where(d == 0, c, back)


def _tri_masks(chunk):
    lower = np.tril(np.ones((chunk, chunk), np.float32))
    return jnp.asarray(np.stack([np.stack([lower, lower.T]), np.stack([lower.T, lower])]))


def _split_hi_lo(x):
    hi = x.astype(jnp.bfloat16)
    lo = (x - hi.astype(F32)).astype(jnp.bfloat16)
    return hi, lo


def _gla_kernel(q_ref, k_ref, v_ref, r_ref, wg_ref, bg_ref, tri_ref, o_ref, s_ref, *, dk, dv, hb):
    d = pl.program_id(0)

    @pl.when(pl.program_id(2) == 0)
    def _():
        s_ref[...] = jnp.zeros_like(s_ref)

    w = hb * dk
    t = q_ref.shape[0]
    hh = t // 2
    x = jnp.dot(r_ref[...].astype(MXU_DTYPE), wg_ref[0].astype(MXU_DTYPE), preferred_element_type=F32) + bg_ref[0]
    la = _log_sigmoid(x) * (1.0 / GLA_TAU)
    tri = tri_ref[0, 0]
    vis = tri > 0
    hi, lo = _split_hi_lo(la)
    bb = jnp.dot(tri.astype(jnp.bfloat16), jnp.concatenate([hi, lo], axis=1), preferred_element_type=F32)
    b = bb[:, :w] + bb[:, w:]
    tot_a = jnp.sum(la[:hh], axis=0, keepdims=True)
    tot_b = jnp.sum(la[hh:], axis=0, keepdims=True)
    tot = tot_a + tot_b
    fwd = d == 0
    mid_a = jnp.where(fwd, 0.5 * tot_a, tot_b + 0.5 * tot_a)
    mid_b = jnp.where(fwd, tot_a + 0.5 * tot_b, 0.5 * tot_b)
    cross = jnp.where(fwd, tot_a, tot_b)
    ref = jnp.concatenate([jnp.broadcast_to(mid_a, (hh, w)), jnp.broadcast_to(mid_b, (t - hh, w))], axis=0)
    q = q_ref[...].astype(F32) * dk ** -0.5
    k = k_ref[...].astype(F32)
    qe = (q * jnp.exp(b - ref)).astype(MXU_DTYPE)
    ke = (k * jnp.exp(ref - b)).astype(MXU_DTYPE)
    qc = (q * jnp.exp(b - cross)).astype(MXU_DTYPE)
    kc = (k * jnp.exp(cross - b)).astype(MXU_DTYPE)
    qb = (q * jnp.exp(b)).astype(MXU_DTYPE)
    kd = (k * jnp.exp(tot - b)).astype(MXU_DTYPE)
    dec = jnp.exp(tot)
    nt = (((1,), (1,)), ((), ()))
    for h in range(hb):
        ks = slice(h * dk, (h + 1) * dk)
        vs = slice(h * dv, (h + 1) * dv)
        v = v_ref[:, vs].astype(MXU_DTYPE)
        blk = lambda qq, kk, rq, rk: lax.dot_general(qq[rq, ks], kk[rk, ks], nt, preferred_element_type=F32)
        ra, rb = slice(0, hh), slice(hh, t)
        att = jnp.concatenate([
            jnp.concatenate([blk(qe, ke, ra, ra), blk(qc, kc, ra, rb)], axis=1),
            jnp.concatenate([blk(qc, kc, rb, ra), blk(qe, ke, rb, rb)], axis=1)], axis=0)
        att = jnp.where(vis, att, 0.0).astype(MXU_DTYPE)
        st = s_ref[h]
        o_ref[0, :, vs] = (jnp.dot(att, v, preferred_element_type=F32)
                           + lax.dot_general(qb[:, ks], st.astype(MXU_DTYPE), nt, preferred_element_type=F32)
                           ).astype(o_ref.dtype)
        upd = lax.dot_general(v, kd[:, ks], (((0,), (0,)), ((), ())), preferred_element_type=F32)
        s_ref[h] = st * dec[:, ks] + upd


def gla_scan(proj, r, w_gate_up, b_gate, ctx_len):
    m = proj.shape[0]
    n_dir, rank, qk = w_gate_up.shape
    dk = qk // HEADS
    dv = 2 * dk
    hb = HEAD_BLOCK
    nhb = HEADS // hb
    n_chunks = m // CHUNK
    n_ctx = ctx_len // CHUNK
    wg = jnp.zeros((n_dir, GATE_PAD, qk), F32)
    for dr in range(n_dir):
        wg = wg.at[dr, dr * rank:(dr + 1) * rank].set(w_gate_up[dr])
    cidx = functools.partial(_chunk_index, n_ctx_chunks=n_ctx, n_chunks=n_chunks)
    return pl.pallas_call(
        functools.partial(_gla_kernel, dk=dk, dv=dv, hb=hb),
        grid=(n_dir, nhb, n_chunks),
        in_specs=[
            pl.BlockSpec((CHUNK, hb * dk), lambda d, h, c: (cidx(d, c), h)),
            pl.BlockSpec((CHUNK, hb * dk), lambda d, h, c: (cidx(d, c), nhb + h)),
            pl.BlockSpec((CHUNK, hb * dv), lambda d, h, c: (cidx(d, c), nhb + h)),
            pl.BlockSpec((CHUNK, GATE_PAD), lambda d, h, c: (cidx(d, c), 0)),
            pl.BlockSpec((1, GATE_PAD, hb * dk), lambda d, h, c: (d, 0, h)),
            pl.BlockSpec((1, 1, hb * dk), lambda d, h, c: (d, 0, h)),
            pl.BlockSpec((1, 1, CHUNK, CHUNK), lambda d, h, c: (d, 0, 0, 0)),
        ],
        out_specs=pl.BlockSpec((1, CHUNK, hb * dv), lambda d, h, c: (d, cidx(d, c), h)),
        out_shape=jax.ShapeDtypeStruct((n_dir, m, dv * HEADS), MXU_DTYPE),
        scratch_shapes=[pltpu.VMEM((hb, dv, dk), F32)],
        compiler_params=_cparams("arbitrary", "arbitrary", "arbitrary"),
        name="gla_scan",
    )(proj, proj, proj, r, wg, b_gate.reshape(n_dir, 1, qk), _tri_masks(CHUNK))


def _mlstm_kernel(q_ref, k_ref, v_ref, gcol_ref, grow_ref, tri_ref, o_ref, c_ref, m_ref, *, dk, dv, hb):
    d = pl.program_id(0)
    h0 = pl.program_id(1) * hb

    @pl.when(pl.program_id(2) == 0)
    def _():
        c_ref[...] = jnp.zeros_like(c_ref)
        m_ref[...] = jnp.zeros_like(m_ref)

    t = q_ref.shape[0]
    tri = tri_ref[0, 0]
    tri_t = tri_ref[0, 1]
    vis = tri > 0
    lane = lax.broadcasted_iota(jnp.int32, (1, gcol_ref.shape[1]), 1)
    gcol = gcol_ref[...]
    row = lax.broadcasted_iota(jnp.int32, (t, 1), 0)
    last = jnp.where(d == 0, t - 1, 0)
    e0 = (lax.broadcasted_iota(jnp.int32, (t, 128), 1) == 0).astype(MXU_DTYPE)
    for h in range(hb):
        ks = slice(h * dk, (h + 1) * dk)
        vs = slice(h * dv, (h + 1) * dv)
        col_i = 2 * d * HEADS + h0 + h
        li_c = jnp.sum(jnp.where(lane == col_i, gcol, 0.0), axis=1, keepdims=True)
        lf_c = _log_sigmoid(jnp.sum(jnp.where(lane == col_i + HEADS, gcol, 0.0), axis=1, keepdims=True))
        li_r = grow_ref[0, pl.ds(col_i, 1), :]
        lf_r = _log_sigmoid(grow_ref[0, pl.ds(col_i + HEADS, 1), :])
        b_c = jnp.sum(tri * lf_r, axis=1, keepdims=True)
        b_r = jnp.sum(tri_t * lf_c, axis=0, keepdims=True)
        b_last = jnp.sum(lf_r, axis=1, keepdims=True)
        m0 = m_ref[h][:, 0:1]
        logw = jnp.where(vis, b_c - b_r + li_r, -jnp.inf)
        w_inter = b_c + m0
        m_t = jnp.maximum(w_inter, jnp.max(logw, axis=1, keepdims=True))
        m_new = jnp.max(jnp.where(row == last, m_t, -jnp.inf), axis=0, keepdims=True)
        carry = jnp.exp(b_last + m0 - m_new)
        w_last_c = jnp.exp(b_last - b_c + li_c - m_new)
        q = q_ref[:, ks].astype(MXU_DTYPE)
        kf = k_ref[:, ks].astype(F32) * dk ** -0.5
        v_ext = jnp.concatenate([v_ref[:, vs].astype(MXU_DTYPE), e0], axis=1)
        c0 = c_ref[h]
        qk = lax.dot_general(q, kf.astype(MXU_DTYPE), (((1,), (1,)), ((), ())), preferred_element_type=F32)
        qk = qk * jnp.exp(logw - m_t)
        inter = jnp.exp(w_inter - m_t)
        num = (jnp.dot(qk.astype(MXU_DTYPE), v_ext, preferred_element_type=F32)
               + inter * jnp.dot(q, c0.astype(MXU_DTYPE), preferred_element_type=F32))
        den = num[:, dv:dv + 1]
        o_ref[0, :, vs] = (num[:, :dv] / jnp.maximum(jnp.abs(den), jnp.exp(-m_t))).astype(o_ref.dtype)
        kw = (kf * w_last_c).astype(MXU_DTYPE)
        c_ref[h] = carry * c0 + lax.dot_general(kw, v_ext, (((0,), (0,)), ((), ())), preferred_element_type=F32)
        m_ref[h] = jnp.broadcast_to(m_new, m_ref.shape[1:])


def mlstm_scan(proj, gates, ctx_len):
    m = proj.shape[0]
    dk = proj.shape[1] // (6 * HEADS)
    dv = 2 * dk
    hb = HEAD_BLOCK
    nhb = HEADS // hb
    n_chunks = m // ML_CHUNK
    n_ctx = ctx_len // ML_CHUNK
    n_gate = 4 * HEADS
    grow = jnp.transpose(gates[:, :n_gate].reshape(n_chunks, ML_CHUNK, n_gate), (0, 2, 1))
    cidx = functools.partial(_chunk_index, n_ctx_chunks=n_ctx, n_chunks=n_chunks)
    return pl.pallas_call(
        functools.partial(_mlstm_kernel, dk=dk, dv=dv, hb=hb),
        grid=(2, nhb, n_chunks),
        in_specs=[
            pl.BlockSpec((ML_CHUNK, hb * dk), lambda d, h, c: (cidx(d, c), h)),
            pl.BlockSpec((ML_CHUNK, hb * dk), lambda d, h, c: (cidx(d, c), nhb + h)),
            pl.BlockSpec((ML_CHUNK, hb * dv), lambda d, h, c: (cidx(d, c), nhb + h)),
            pl.BlockSpec((ML_CHUNK, GATE_PAD), lambda d, h, c: (cidx(d, c), 0)),
            pl.BlockSpec((1, n_gate, ML_CHUNK), lambda d, h, c: (cidx(d, c), 0, 0)),
            pl.BlockSpec((1, 2, ML_CHUNK, ML_CHUNK), lambda d, h, c: (d, 0, 0, 0)),
        ],
        out_specs=pl.BlockSpec((1, ML_CHUNK, hb * dv), lambda d, h, c: (d, cidx(d, c), h)),
        out_shape=jax.ShapeDtypeStruct((2, m, dv * HEADS), MXU_DTYPE),
        scratch_shapes=[pltpu.VMEM((hb, dk, dv + 128), F32), pltpu.VMEM((hb, 1, 128), F32)],
        compiler_params=_cparams("arbitrary", "arbitrary", "arbitrary"),
        name="mlstm_scan",
    )(proj, proj, proj, gates, grow, _tri_masks(ML_CHUNK))


def _pad_cols(w, width):
    return jnp.pad(w, ((0, 0), (0, width - w.shape[1])))


def _s5_layer(u, p, j, ctx_len):
    weights = _s5_weights(p['s5_lam_re'][j], p['s5_lam_im'][j], p['s5_log_step'][j], p['s5_b_re'][j],
                          p['s5_b_im'][j], p['s5_c_re'][j], p['s5_c_im'][j])
    act = s5_scan_act(u, p['s5_d'][j], weights, ctx_len)
    w_glu = p['s5_w_glu']
    return matmul(act, cast_cols(w_glu, j, w_glu.shape[2], MXU_DTYPE), p['s5_b_glu'][j], MXU_DTYPE)


def _gla_layer(u, p, j, ctx_len):
    w_in = p['gla_w_in'][j]
    n_main = w_in.shape[1] - 2 * p['gla_w_gate_up'].shape[2]
    proj = matmul(u, cast_cols(p['gla_w_in'], j, n_main, MXU_DTYPE), jnp.zeros((n_main,), F32), MXU_DTYPE)
    r = matmul(u, _pad_cols(w_in[:, n_main:], GATE_PAD).astype(MXU_DTYPE), jnp.zeros((GATE_PAD,), F32), F32)
    o2 = gla_scan(proj, r, p['gla_w_gate_up'][j], p['gla_b_gate'][j], ctx_len)
    on = head_gate(o2, proj, 2, p['gla_norm_g'][j], sigmoid_gate=False)
    d = u.shape[1]
    return matmul(on, cast_cols(p['gla_w_out'], j, d, MXU_DTYPE), jnp.zeros((d,), F32), MXU_DTYPE)


def _mlstm_layer(u, p, j, ctx_len):
    w_in = p['mlstm_w_in'][j]
    n_gate = 4 * HEADS
    n_main = w_in.shape[1] - n_gate
    proj = matmul(u, cast_cols(p['mlstm_w_in'], j, n_main, MXU_DTYPE), jnp.zeros((n_main,), F32), MXU_DTYPE)
    b_gate = jnp.pad(p['mlstm_b_gate'][j].reshape(n_gate), (0, GATE_PAD - n_gate))
    gates = matmul(u, _pad_cols(w_in[:, n_main:], GATE_PAD).astype(MXU_DTYPE), b_gate, F32)
    h2 = mlstm_scan(proj, gates, ctx_len)
    hn = head_gate(h2, proj, 2, p['mlstm_norm_g'][j], sigmoid_gate=True)
    d = u.shape[1]
    return matmul(hn, cast_cols(p['mlstm_w_out'], j, d, MXU_DTYPE), jnp.zeros((d,), F32), MXU_DTYPE)


def kernel(x, c, ctx, c_ctx, ada_w_down, ada_w_up, ada_b, norm_g, ffn_w_in, ffn_conv_w, ffn_conv_b, ffn_w_out, s5_lam_re, s5_lam_im, s5_log_step, s5_b_re, s5_b_im, s5_c_re, s5_c_im, s5_d, s5_w_glu, s5_b_glu, gla_w_in, gla_w_gate_up, gla_b_gate, gla_norm_g, gla_w_out, mlstm_w_in, mlstm_b_gate, mlstm_norm_g, mlstm_w_out):
    p = dict(s5_lam_re=s5_lam_re, s5_lam_im=s5_lam_im, s5_log_step=s5_log_step, s5_b_re=s5_b_re, s5_b_im=s5_b_im,
             s5_c_re=s5_c_re, s5_c_im=s5_c_im, s5_d=s5_d, s5_w_glu=s5_w_glu, s5_b_glu=s5_b_glu,
             gla_w_in=gla_w_in, gla_w_gate_up=gla_w_gate_up, gla_b_gate=gla_b_gate, gla_norm_g=gla_norm_g,
             gla_w_out=gla_w_out, mlstm_w_in=mlstm_w_in, mlstm_b_gate=mlstm_b_gate, mlstm_norm_g=mlstm_norm_g,
             mlstm_w_out=mlstm_w_out)
    bsz, seq, d = x.shape
    ctx_len = ctx.shape[1]
    depth = ada_w_down.shape[0]
    assert bsz == 1 and ctx_len == ROW_TILE and seq % (GRID_W * (ROW_TILE // GRID_W)) == 0
    cond = jnp.zeros((8, d), F32).at[0].set(c_ctx).at[1].set(c[0])
    mods = ada_all(cond, ada_w_down, ada_w_up, ada_b)[:, :2].reshape(depth, 2, 6, 1, d)
    mod = lambda i, n: mods[i, :, n]
    mixer_dtype = lambda i: F32 if i % 3 == 0 else MXU_DTYPE
    h = jnp.concatenate([ctx[0], x[0]], axis=0)
    u = norm_mod(h, norm_g[0, 0], mod(0, 1), mod(0, 0), ctx_len, mixer_dtype(0))
    f2 = ffn_w_in.shape[2]
    for i in range(depth):
        kind, j = i % 3, i // 3
        ffn_in = (norm_g[i, 2], mod(i, 4), mod(i, 3), MXU_DTYPE)
        if kind == 0:
            h, uf = resid(h, _s5_layer(u, p, j, ctx_len), norm_g[i, 1], mod(i, 2), ctx_len, glu=True, nxt=ffn_in)
        else:
            y = _gla_layer(u, p, j, ctx_len) if kind == 1 else _mlstm_layer(u, p, j, ctx_len)
            h, uf = resid(h, y, norm_g[i, 1], mod(i, 2), ctx_len, nxt=ffn_in)
        gu = matmul(uf, ffn_w_in[i].astype(MXU_DTYPE), jnp.zeros((f2,), F32), MXU_DTYPE)
        act = conv_act(gu, ffn_conv_w[i], ffn_conv_b[i], ctx_len)
        f = matmul(act, ffn_w_out[i].astype(MXU_DTYPE), jnp.zeros((d,), F32), MXU_DTYPE)
        if i + 1 < depth:
            nxt = (norm_g[i + 1, 0], mod(i + 1, 1), mod(i + 1, 0), mixer_dtype(i + 1))
            h, u = resid(h, f, norm_g[i, 3], mod(i, 5), ctx_len, nxt=nxt)
        else:
            h = resid(h, f, norm_g[i, 3], mod(i, 5), ctx_len, drop_ctx=True)
    return h.reshape(bsz, seq, d)
```
